```python
import jax, jax.numpy as jnp
from jax import lax
import numpy as np

D_MODEL = 2048
BATCH = 8
SEQ = 2048
DEPTH = 2
DEC_BATCH = 32
DEC_SEQ = 64
PAST_LEN = 1024

CHUNK = 64
N_META = 16
EPS = 1e-6

MLSTM_H = 8
MLSTM_DH = 128
MLSTM_W = MLSTM_H * MLSTM_DH
HGRN_H = 4
HGRN_DK = 128
HGRN_DV = 128
HGRN_W = HGRN_H * HGRN_DV
RWKV_H = 8
RWKV_DH = 64
RWKV_W = RWKV_H * RWKV_DH
RWKV_RANK_W = 64
RWKV_RANK_A = 64
RWKV_GN_EPS = 64e-5
N_BRANCH = 3

A_SIZES = (MLSTM_W, MLSTM_W, MLSTM_W, MLSTM_H, MLSTM_H, MLSTM_W, MLSTM_W)
B_SIZES = (HGRN_H * HGRN_DK, HGRN_H * HGRN_DK, HGRN_W, HGRN_W)
C_SIZES = (RWKV_W, RWKV_W, RWKV_W, RWKV_RANK_W, RWKV_RANK_A, RWKV_W)
G_SIZES = (D_MODEL,) * N_BRANCH
A_COLS = sum(A_SIZES)
B_COLS = sum(B_SIZES)
C_COLS = sum(C_SIZES)
G_COLS = sum(G_SIZES)
N_IN = A_COLS + B_COLS + C_COLS + G_COLS

kernel_name = 'hybrid_mlstm_hgrn2_rwkv7_stream_step'


def f32(a):
    return a.astype(jnp.float32)


def split_cols(p, sizes):
    idx = [int(i) for i in np.cumsum(sizes)[:-1]]
    return jnp.split(p, idx, axis=-1)


def rmsnorm(x, g):
    xf = f32(x)
    y = xf * lax.rsqrt(jnp.mean(xf * xf, axis=-1, keepdims=True) + EPS)
    return (y * f32(g)).astype(x.dtype)


def heads_first(a, n_heads):
    b, t, _ = a.shape
    return a.reshape(b, t, n_heads, -1).transpose(0, 2, 1, 3)


def to_chunks(a, length):
    b, h, t = a.shape[:3]
    a = a.reshape((b, h, t // length, length) + a.shape[3:])
    return jnp.moveaxis(a, 2, 0)


def from_chunks(a):
    a = jnp.moveaxis(a, 0, 2)
    return a.reshape(a.shape[:2] + (-1,) + a.shape[4:])


def run_segments(step, state, xs, segments):
    outs = []
    for t0, t1, length in segments:
        seg = tuple(to_chunks(a[:, :, t0:t1], length) for a in xs)
        state, out = lax.scan(step, state, seg)
        outs.append(from_chunks(out))
    return state, jnp.concatenate(outs, axis=2)


def mlstm_step(carry, inp):
    c, n, m = carry
    q, k, v, ig, lf = inp
    length = q.shape[2]
    causal = jnp.tril(jnp.ones((length, length), dtype=bool))
    b = jnp.cumsum(lf, axis=-1)
    d = jnp.where(causal, b[..., :, None] - b[..., None, :] + ig[..., None, :], -jnp.inf)
    inter = b + m[..., None]
    m_t = jnp.maximum(inter, jnp.max(d, axis=-1))
    w_inter = jnp.exp(inter - m_t)
    s = jnp.einsum('bhtk,bhsk->bhts', q, k) * jnp.exp(d - m_t[..., None])
    num = w_inter[..., None] * jnp.einsum('bhvk,bhtk->bhtv', c, q) + jnp.einsum('bhts,bhsv->bhtv', s, v)
    den = w_inter * jnp.einsum('bhk,bhtk->bht', n, q) + jnp.sum(s, axis=-1)
    h = num / jnp.maximum(jnp.abs(den), jnp.exp(-m_t))[..., None]
    b_end = b[..., -1]
    g = b_end[..., None] - b + ig
    m_new = jnp.maximum(b_end + m, jnp.max(g, axis=-1))
    w_old = jnp.exp(b_end + m - m_new)
    w_s = jnp.exp(g - m_new[..., None])
    c_new = w_old[..., None, None] * c + jnp.einsum('bhs,bhsv,bhsk->bhvk', w_s, v, k)
    n_new = w_old[..., None] * n + jnp.einsum('bhs,bhsk->bhk', w_s, k)
    return (c_new, n_new, m_new), h


def hgrn_step(s_state, inp):
    q, k, v, lf = inp
    length = q.shape[2]
    causal = jnp.tril(jnp.ones((length, length), dtype=bool))[:, :, None]
    a = jnp.cumsum(lf, axis=2)
    rel = jnp.where(causal, a[:, :, :, None, :] - a[:, :, None, :, :], -jnp.inf)
    scores = jnp.einsum('bhtk,bhsk,bhtsk->bhts', q, k, jnp.exp(rel))
    o = jnp.einsum('bhtk,bhkv->bhtv', q * jnp.exp(a), s_state) + jnp.einsum('bhts,bhsv->bhtv', scores, v)
    a_end = a[:, :, -1:]
    s_new = jnp.exp(a_end[:, :, 0])[..., None] * s_state + jnp.einsum('bhsk,bhsv->bhkv', k * jnp.exp(a_end - a), v)
    return s_new, o


def rwkv_step(s_state, inp):
    r, w, k, v, kk, a = inp
    sa = jnp.einsum('bhvk,bhk->bhv', s_state, -kk)
    s_new = (s_state * w[:, :, None, :] + sa[..., None] * (kk * a)[:, :, None, :]
             + v[..., None] * k[:, :, None, :])
    y = jnp.einsum('bhvk,bhk->bhv', s_new, r)
    return s_new, y


def mlstm_branch(pa, p, state, segments):
    bsz, t, _ = pa.shape
    q, k, v, ig, fg, og, z = split_cols(f32(pa), A_SIZES)
    q = heads_first(q, MLSTM_H)
    k = heads_first(k, MLSTM_H) * (MLSTM_DH ** -0.5)
    v = heads_first(v, MLSTM_H)
    ig = (ig + p['mlstm_b_i']).transpose(0, 2, 1)
    lf = jax.nn.log_sigmoid(fg + p['mlstm_b_f']).transpose(0, 2, 1)
    state, h = run_segments(mlstm_step, tuple(f32(s) for s in state), (q, k, v, ig, lf), segments)
    h = h.transpose(0, 2, 1, 3)
    mu = jnp.mean(h, axis=-1, keepdims=True)
    var = jnp.mean(jnp.square(h - mu), axis=-1, keepdims=True)
    h = ((h - mu) * lax.rsqrt(var + EPS)).reshape(bsz, t, MLSTM_W) * p['mlstm_norm']
    return h * jax.nn.sigmoid(og) * jax.nn.silu(z), state


def hgrn_branch(pb, p, lb, s_state, segments):
    bsz, t, _ = pb.shape
    q, fp, i, z = split_cols(f32(pb), B_SIZES)
    f = lb + (1.0 - lb) * jax.nn.sigmoid(fp)
    lf = heads_first(jnp.log(f), HGRN_H)
    k = heads_first((1.0 - lb) * jax.nn.sigmoid(-fp), HGRN_H)
    s_state, o = run_segments(hgrn_step, f32(s_state),
                              (heads_first(q, HGRN_H), k, heads_first(i, HGRN_H), lf), segments)
    o = o.transpose(0, 2, 1, 3)
    o = o * lax.rsqrt(jnp.mean(o * o, axis=-1, keepdims=True) + EPS)
    return o.reshape(bsz, t, HGRN_W) * p['hgrn_norm'] * jax.nn.silu(z), s_state


def rwkv_branch(pc, p, shift, s_state):
    pc = f32(pc)
    bsz, t, _ = pc.shape
    prev = jnp.concatenate([f32(shift), pc[:, :-1]], axis=1)
    xs = pc + p['rwkv_mu'] * (prev - pc)
    r, k, v, wd, ad, z = split_cols(xs, C_SIZES)
    w_logit = -jax.nn.softplus(-(p['rwkv_w0'] + jnp.tanh(wd) @ p['rwkv_w_up'])) - 0.5
    decay = jnp.exp(-jnp.exp(w_logit))
    a = jax.nn.sigmoid(p['rwkv_a0'] + ad @ p['rwkv_a_up'])

    def split_h(u):
        return u.reshape(bsz, t, RWKV_H, RWKV_DH)

    def time_major(u):
        return u.transpose(1, 0, 2, 3)

    kk = split_h(k * p['rwkv_k_k'])
    kk = kk / jnp.maximum(jnp.sqrt(jnp.sum(kk * kk, axis=-1, keepdims=True)), 1e-12)
    k = k * (1.0 + (a - 1.0) * p['rwkv_k_a'])
    r, k, v, decay, a = (split_h(u) for u in (r, k, v, decay, a))
    s_state, y = lax.scan(rwkv_step, f32(s_state), tuple(time_major(u) for u in (r, decay, k, v, kk, a)))
    y = time_major(y)
    mu = jnp.mean(y, axis=-1, keepdims=True)
    var = jnp.mean(jnp.square(y - mu), axis=-1, keepdims=True)
    y = ((y - mu) * lax.rsqrt(var + RWKV_GN_EPS) * p['rwkv_gn_g'].reshape(RWKV_H, RWKV_DH)
         + p['rwkv_gn_b'].reshape(RWKV_H, RWKV_DH))
    bonus = jnp.sum(r * k * p['rwkv_r_k'].reshape(RWKV_H, RWKV_DH), axis=-1, keepdims=True) * v
    out = (y + bonus).reshape(bsz, t, RWKV_W) * jax.nn.silu(z)
    return out, pc[:, -1:], s_state


def layer(x, p, lb, state, segments):
    c, n, m, s_h, s_r, shift = state
    h = rmsnorm(x, p['norm_pre'])
    proj = h @ p['w_in']
    pa, pb, pc, pg = split_cols(proj, (A_COLS, B_COLS, C_COLS, G_COLS))
    ya, (c, n, m) = mlstm_branch(pa, p, (c, n, m), segments)
    yb, s_h = hgrn_branch(pb, p, lb, s_h, segments)
    yc, shift, s_r = rwkv_branch(pc, p, shift, s_r)
    ga, gb, gc = split_cols(jax.nn.sigmoid(f32(pg)), G_SIZES)
    dt = h.dtype
    merged = (ga * (ya.astype(dt) @ p['w_proj_a']) + gb * (yb.astype(dt) @ p['w_proj_b'])
              + gc * (yc.astype(dt) @ p['w_proj_c']))
    y = merged.astype(dt) @ p['w_out']
    return x + rmsnorm(y, p['norm_post']), (c, n, m, s_h, s_r, shift)


def setup_inputs(seed: int = 0) -> dict:
    key = jax.random.key(seed)
    ks = jax.random.split(key, 40)

    def nrm(i, shape, scale=1.0):
        return scale * jax.random.normal(ks[i], shape, jnp.float32)

    return {
        'x_prompt': nrm(0, (BATCH, SEQ, D_MODEL)),
        'x_sample': nrm(1, (DEC_BATCH, DEC_SEQ, D_MODEL)),
        'state_mlstm_C': nrm(2, (DEPTH, DEC_BATCH, MLSTM_H, MLSTM_DH, MLSTM_DH), 0.5),
        'state_mlstm_n': nrm(3, (DEPTH, DEC_BATCH, MLSTM_H, MLSTM_DH), 0.5),
        'state_mlstm_m': nrm(4, (DEPTH, DEC_BATCH, MLSTM_H), 0.5),
        'state_hgrn_S': nrm(5, (DEPTH, DEC_BATCH, HGRN_H, HGRN_DK, HGRN_DV), 0.5),
        'state_rwkv_S': nrm(6, (DEPTH, DEC_BATCH, RWKV_H, RWKV_DH, RWKV_DH), 0.5),
        'cache_rwkv_shift': nrm(7, (DEPTH, DEC_BATCH, 1, C_COLS)),
        'meta_tokens': nrm(8, (N_META, D_MODEL)),
        'norm_pre': 1.0 + nrm(9, (DEPTH, D_MODEL), 0.05),
        'norm_post': 1.0 + nrm(10, (DEPTH, D_MODEL), 0.05),
        'w_in': nrm(11, (DEPTH, D_MODEL, N_IN), D_MODEL ** -0.5),
        'mlstm_b_i': nrm(12, (DEPTH, MLSTM_H), 0.1),
        'mlstm_b_f': 3.0 + nrm(13, (DEPTH, MLSTM_H), 0.5),
        'mlstm_norm': 1.0 + nrm(14, (DEPTH, MLSTM_W), 0.05),
        'hgrn_lb_logits': nrm(15, (DEPTH, HGRN_H * HGRN_DK), 0.5),
        'hgrn_norm': 1.0 + nrm(16, (DEPTH, HGRN_W), 0.05),
        'rwkv_mu': jax.random.uniform(ks[17], (DEPTH, C_COLS), jnp.float32),
        'rwkv_w0': -2.0 + nrm(18, (DEPTH, RWKV_W), 0.5),
        'rwkv_w_up': nrm(19, (DEPTH, RWKV_RANK_W, RWKV_W), 0.5 * RWKV_RANK_W ** -0.5),
        'rwkv_a0': nrm(20, (DEPTH, RWKV_W), 0.1),
        'rwkv_a_up': nrm(21, (DEPTH, RWKV_RANK_A, RWKV_W), 0.5 * RWKV_RANK_A ** -0.5),
        'rwkv_k_k': 0.85 + nrm(22, (DEPTH, RWKV_W), 0.05),
        'rwkv_k_a': 1.0 + nrm(23, (DEPTH, RWKV_W), 0.05),
        'rwkv_r_k': nrm(24, (DEPTH, RWKV_W), 0.1),
        'rwkv_gn_g': 1.0 + nrm(25, (DEPTH, RWKV_W), 0.05),
        'rwkv_gn_b': nrm(26, (DEPTH, RWKV_W), 0.01),
        'w_proj_a': nrm(27, (DEPTH, MLSTM_W, D_MODEL), MLSTM_W ** -0.5),
        'w_proj_b': nrm(28, (DEPTH, HGRN_W, D_MODEL), HGRN_W ** -0.5),
        'w_proj_c': nrm(29, (DEPTH, RWKV_W, D_MODEL), RWKV_W ** -0.5),
        'w_out': nrm(30, (DEPTH, D_MODEL, D_MODEL), D_MODEL ** -0.5),
    }


def reference(x_prompt, x_sample, state_mlstm_C, state_mlstm_n, state_mlstm_m, state_hgrn_S,
              state_rwkv_S, cache_rwkv_shift, meta_tokens, norm_pre, norm_post, w_in,
              mlstm_b_i, mlstm_b_f, mlstm_norm, hgrn_lb_logits, hgrn_norm, rwkv_mu, rwkv_w0,
              rwkv_w_up, rwkv_a0, rwkv_a_up, rwkv_k_k, rwkv_k_a, rwkv_r_k, rwkv_gn_g, rwkv_gn_b,
              w_proj_a, w_proj_b, w_proj_c, w_out):
    bp, t_p, _ = x_prompt.shape
    t_s = x_sample.shape[1]
    sm = jax.nn.softmax(f32(hgrn_lb_logits), axis=0)
    lb_all = jnp.cumsum(sm, axis=0) - sm[0]

    meta = jnp.broadcast_to(meta_tokens.astype(x_prompt.dtype)[None], (bp, N_META, D_MODEL))
    x_p = jnp.concatenate([meta, x_prompt], axis=1)
    x_s = x_sample
    seg_p = ((0, N_META, N_META), (N_META, N_META + t_p, CHUNK))
    seg_s = ((0, t_s, t_s),)
    init_p = (jnp.zeros((bp, MLSTM_H, MLSTM_DH, MLSTM_DH), jnp.float32),
              jnp.zeros((bp, MLSTM_H, MLSTM_DH), jnp.float32),
              jnp.zeros((bp, MLSTM_H), jnp.float32),
              jnp.zeros((bp, HGRN_H, HGRN_DK, HGRN_DV), jnp.float32),
              jnp.zeros((bp, RWKV_H, RWKV_DH, RWKV_DH), jnp.float32),
              jnp.zeros((bp, 1, C_COLS), jnp.float32))
    outs_p = []
    outs_s = []
    for l in range(DEPTH):
        p = {
            'norm_pre': norm_pre[l], 'norm_post': norm_post[l], 'w_in': w_in[l],
            'mlstm_b_i': f32(mlstm_b_i[l]), 'mlstm_b_f': f32(mlstm_b_f[l]), 'mlstm_norm': f32(mlstm_norm[l]),
            'hgrn_norm': f32(hgrn_norm[l]),
            'rwkv_mu': f32(rwkv_mu[l]), 'rwkv_w0': f32(rwkv_w0[l]), 'rwkv_w_up': f32(rwkv_w_up[l]),
            'rwkv_a0': f32(rwkv_a0[l]), 'rwkv_a_up': f32(rwkv_a_up[l]), 'rwkv_k_k': f32(rwkv_k_k[l]),
            'rwkv_k_a': f32(rwkv_k_a[l]), 'rwkv_r_k': f32(rwkv_r_k[l]), 'rwkv_gn_g': f32(rwkv_gn_g[l]),
            'rwkv_gn_b': f32(rwkv_gn_b[l]),
            'w_proj_a': w_proj_a[l], 'w_proj_b': w_proj_b[l], 'w_proj_c': w_proj_c[l], 'w_out': w_out[l],
        }
        x_p, st_p = layer(x_p, p, lb_all[l], init_p, seg_p)
        st_in = (state_mlstm_C[l], state_mlstm_n[l], state_mlstm_m[l], state_hgrn_S[l],
                 state_rwkv_S[l], cache_rwkv_shift[l])
        x_s, st_s = layer(x_s, p, lb_all[l], st_in, seg_s)
        outs_p.append(st_p)
        outs_s.append(st_s)

    mlstm_C_p, mlstm_n_p, mlstm_m_p, hgrn_S_p, rwkv_S_p, rwkv_shift_p = (
        jnp.stack([o[j] for o in outs_p]) for j in range(6))
    mlstm_C_s, mlstm_n_s, mlstm_m_s, hgrn_S_s, rwkv_S_s, rwkv_shift_s = (
        jnp.stack([o[j] for o in outs_s]) for j in range(6))
    y_prompt = x_p[:, N_META:]
    y_sample = x_s
    return (y_prompt, y_sample, mlstm_C_p, mlstm_n_p, mlstm_m_p, hgrn_S_p, rwkv_S_p, rwkv_shift_p,
            mlstm_C_s, mlstm_n_s, mlstm_m_s, hgrn_S_s, rwkv_S_s, rwkv_shift_s)
```

```python
import functools
import math

import jax
import jax.numpy as jnp
from jax import lax
from jax.experimental import pallas as pl
from jax.experimental.pallas import tpu as pltpu

F32 = jnp.float32
BF16 = jnp.bfloat16

D_MODEL = 2048
CHUNK = 64
N_META = 16
EPS = 1e-6

MLSTM_H = 8
MLSTM_DH = 128
MLSTM_W = MLSTM_H * MLSTM_DH
HGRN_H = 4
HGRN_DK = 128
HGRN_DV = 128
HGRN_W = HGRN_H * HGRN_DV
RWKV_H = 8
RWKV_DH = 64
RWKV_W = RWKV_H * RWKV_DH
RWKV_RANK = 64
RWKV_GN_EPS = 64e-5
C_COLS = 4 * RWKV_W + 2 * RWKV_RANK

LANES = 128
HGRN_SUB = 16
A_W = 5 * MLSTM_W
B_W = 4 * HGRN_W
C_W = C_COLS + LANES
IF_BLOCK = C_COLS // LANES
MERGE_NC = 256
VMEM_LIMIT = 56 * 1024 * 1024


def _dot(a, b):
    return jnp.dot(a, b, preferred_element_type=F32)


def _dot_nt(a, b):
    return lax.dot_general(a, b, (((1,), (1,)), ((), ())), preferred_element_type=F32)


def _dot_tn(a, b):
    return lax.dot_general(a, b, (((0,), (0,)), ((), ())), preferred_element_type=F32)


def _bf(a):
    return a.astype(BF16)


def _sigmoid(x):
    return 1.0 / (1.0 + jnp.exp(-x))


def _silu(x):
    return x * _sigmoid(x)


def _softplus(x):
    return jnp.maximum(x, 0.0) + jnp.log1p(jnp.exp(-jnp.abs(x)))


def _tri(length, strict=False):
    row = lax.broadcasted_iota(jnp.int32, (length, length), 0)
    col = lax.broadcasted_iota(jnp.int32, (length, length), 1)
    return (row > col) if strict else (row >= col)


def _cumsum_time(x, tri_b):
    hi = _bf(x)
    r1 = x - hi.astype(F32)
    mid = _bf(r1)
    lo = _bf(r1 - mid.astype(F32))
    return _dot(tri_b, hi) + _dot(tri_b, mid) + _dot(tri_b, lo)


def _rmsnorm(x, g):
    return x * lax.rsqrt(jnp.mean(x * x, axis=-1, keepdims=True) + EPS) * g


def _row_tile(rows, cap):
    best = None
    for t in range(16, min(rows, cap) + 1, 16):
        if rows % t == 0:
            best = t
    assert best is not None, rows
    return best


def _col_tile(cols, cap):
    best = None
    for t in range(LANES, min(cols, cap) + 1, LANES):
        if cols % t == 0:
            best = t
    assert best is not None, cols
    return best


def _proj_kernel(x_ref, g_ref, w_ref, o_ref):
    h = _bf(_rmsnorm(x_ref[...], g_ref[...]))
    o_ref[...] = _dot(h, w_ref[...])


def _proj(x, g, w, name):
    rows, _ = x.shape
    cols = w.shape[1]
    tm = _row_tile(rows, 1024)
    tn = _col_tile(cols, 1280)
    return pl.pallas_call(
        _proj_kernel,
        grid=(cols // tn, rows // tm),
        in_specs=[
            pl.BlockSpec((tm, D_MODEL), lambda j, i: (i, 0)),
            pl.BlockSpec((1, D_MODEL), lambda j, i: (0, 0)),
            pl.BlockSpec((D_MODEL, tn), lambda j, i: (0, j)),
        ],
        out_specs=pl.BlockSpec((tm, tn), lambda j, i: (i, j)),
        out_shape=jax.ShapeDtypeStruct((rows, cols), F32),
        compiler_params=pltpu.CompilerParams(
            dimension_semantics=("parallel", "parallel"), vmem_limit_bytes=VMEM_LIMIT),
        name=name,
    )(x, g, w)


def _merge_kernel(x_ref, gpre_ref, ya_ref, yb_ref, yc_ref, wg_ref, wpa_ref, wpb_ref, wpc_ref,
                  wo_ref, gpost_ref, o_ref, h_scr, acc_scr):
    j = pl.program_id(1)
    nc = MERGE_NC

    @pl.when(j == 0)
    def _():
        h_scr[...] = _bf(_rmsnorm(x_ref[...], gpre_ref[...]))
        acc_scr[...] = jnp.zeros_like(acc_scr)

    gates = _sigmoid(_dot(h_scr[...], wg_ref[...]))
    merged = (gates[:, 0:nc] * _dot(ya_ref[...], wpa_ref[...])
              + gates[:, nc:2 * nc] * _dot(yb_ref[...], wpb_ref[...])
              + gates[:, 2 * nc:3 * nc] * _dot(yc_ref[...], wpc_ref[...]))
    acc_scr[...] += _dot(_bf(merged), wo_ref[...])

    @pl.when(j == pl.num_programs(1) - 1)
    def _():
        o_ref[...] = x_ref[...] + _rmsnorm(acc_scr[...], gpost_ref[...])


def _merge(x, gpre, ya, yb, yc, wg, wpa, wpb, wpc, wo, gpost, name):
    rows = x.shape[0]
    tm = _row_tile(rows, 640)
    nc = MERGE_NC
    row = lambda i, j: (i, 0)
    return pl.pallas_call(
        _merge_kernel,
        grid=(rows // tm, D_MODEL // nc),
        in_specs=[
            pl.BlockSpec((tm, D_MODEL), row),
            pl.BlockSpec((1, D_MODEL), lambda i, j: (0, 0)),
            pl.BlockSpec((tm, MLSTM_W), row),
            pl.BlockSpec((tm, HGRN_W), row),
            pl.BlockSpec((tm, RWKV_W), row),
            pl.BlockSpec((D_MODEL, 3 * nc), lambda i, j: (0, j)),
            pl.BlockSpec((MLSTM_W, nc), lambda i, j: (0, j)),
            pl.BlockSpec((HGRN_W, nc), lambda i, j: (0, j)),
            pl.BlockSpec((RWKV_W, nc), lambda i, j: (0, j)),
            pl.BlockSpec((nc, D_MODEL), lambda i, j: (j, 0)),
            pl.BlockSpec((1, D_MODEL), lambda i, j: (0, 0)),
        ],
        out_specs=pl.BlockSpec((tm, D_MODEL), row),
        out_shape=jax.ShapeDtypeStruct((rows, D_MODEL), F32),
        scratch_shapes=[pltpu.VMEM((tm, D_MODEL), BF16), pltpu.VMEM((tm, D_MODEL), F32)],
        compiler_params=pltpu.CompilerParams(
            dimension_semantics=("parallel", "arbitrary"), vmem_limit_bytes=VMEM_LIMIT),
        name=name,
    )(x, gpre, ya, yb, yc, wg, wpa, wpb, wpc, wo, gpost)


def _mlstm_kernel(q_ref, k_ref, v_ref, o_ref, z_ref, if_ref, bias_ref, norm_ref,
                  c0_ref, n0_ref, m0_ref, y_ref, c_out, n_out, m_out, cn_scr, m_scr, *, length):
    c_idx = pl.program_id(1)
    dh = MLSTM_DH
    lane0 = lax.broadcasted_iota(jnp.int32, (dh, dh), 1) == 0

    @pl.when(c_idx == 0)
    def _():
        n_t = n0_ref[0].T
        for h in range(MLSTM_H):
            cn_scr[h, :, 0:dh] = c0_ref[0, h].T
            cn_scr[h, :, dh:2 * dh] = jnp.where(lane0, n_t[:, h:h + 1], 0.0)
        m_scr[...] = m0_ref[0]

    causal = _tri(length)
    tri_b = _bf(jnp.where(causal, 1.0, 0.0))
    gates = if_ref[...] + bias_ref[...]
    log_f = jnp.minimum(gates, 0.0) - jnp.log1p(jnp.exp(-jnp.abs(gates)))
    b_all = _cumsum_time(log_f, tri_b)
    gates_t = gates.T
    b_all_t = b_all.T
    ones_blk = jnp.where(lax.broadcasted_iota(jnp.int32, (length, dh), 1) == 0, 1.0, 0.0)
    scale = MLSTM_DH ** -0.5

    for h in range(MLSTM_H):
        sl = slice(h * dh, (h + 1) * dh)
        q = q_ref[:, sl]
        k = k_ref[:, sl] * scale
        v = v_ref[:, sl]
        b_col = b_all[:, MLSTM_H + h:MLSTM_H + h + 1]
        ig_col = gates[:, h:h + 1]
        b_row = b_all_t[MLSTM_H + h:MLSTM_H + h + 1, :]
        ig_row = gates_t[h:h + 1, :]
        m_prev = m_scr[h:h + 1, 0:1]

        d = jnp.where(causal, b_col - b_row + ig_row, -jnp.inf)
        inter = b_col + m_prev
        m_t = jnp.maximum(inter, jnp.max(d, axis=-1, keepdims=True))
        w_inter = jnp.exp(inter - m_t)
        qb = _bf(q)
        s = _dot_nt(qb, _bf(k)) * jnp.exp(d - m_t)
        cn = cn_scr[h]
        v1 = _bf(jnp.concatenate([v, ones_blk], axis=1))
        tot = w_inter * _dot(qb, _bf(cn)) + _dot(_bf(s), v1)
        num = tot[:, 0:dh]
        den = tot[:, dh:dh + 1]
        hid = num / jnp.maximum(jnp.abs(den), jnp.exp(-m_t))
        mu = jnp.mean(hid, axis=-1, keepdims=True)
        cen = hid - mu
        var = jnp.mean(cen * cen, axis=-1, keepdims=True)
        hid = cen * lax.rsqrt(var + EPS) * norm_ref[:, sl]
        y_ref[:, sl] = _bf(hid * _sigmoid(o_ref[:, sl]) * _silu(z_ref[:, sl]))

        b_end = b_col[length - 1:length, :]
        g_col = b_end - b_col + ig_col
        m_new = jnp.maximum(b_end + m_prev, jnp.max(g_col, axis=0, keepdims=True))
        w_old = jnp.exp(b_end + m_prev - m_new)
        w_s = jnp.exp(g_col - m_new)
        cn_scr[h] = w_old * cn + _dot_tn(_bf(k * w_s), v1)
        m_scr[h:h + 1, :] = jnp.broadcast_to(m_new, (1, LANES))

    @pl.when(c_idx == pl.num_programs(1) - 1)
    def _():
        for h in range(MLSTM_H):
            c_out[0, h] = cn_scr[h, :, 0:dh].T
            n_out[0, h:h + 1, :] = cn_scr[h, :, dh:2 * dh].T[0:1, :]
        m_out[0] = m_scr[...]


def _mlstm_call(pa, pc, bias, norm, c0, n0, m0, *, n_seq, n_chunks, length, row_off, name):
    assert row_off % length == 0
    blk0 = row_off // length
    rows = n_seq * n_chunks * length

    def col(j):
        return lambda b, c: (blk0 + b * n_chunks + c, j)

    st4 = lambda b, c: (b, 0, 0, 0)
    st3 = lambda b, c: (b, 0, 0)
    return pl.pallas_call(
        functools.partial(_mlstm_kernel, length=length),
        grid=(n_seq, n_chunks),
        in_specs=[
            pl.BlockSpec((length, MLSTM_W), col(0)),
            pl.BlockSpec((length, MLSTM_W), col(1)),
            pl.BlockSpec((length, MLSTM_W), col(2)),
            pl.BlockSpec((length, MLSTM_W), col(3)),
            pl.BlockSpec((length, MLSTM_W), col(4)),
            pl.BlockSpec((length, LANES), col(IF_BLOCK)),
            pl.BlockSpec((1, LANES), lambda b, c: (0, 0)),
            pl.BlockSpec((1, MLSTM_W), lambda b, c: (0, 0)),
            pl.BlockSpec((1, MLSTM_H, MLSTM_DH, MLSTM_DH), st4),
            pl.BlockSpec((1, MLSTM_H, MLSTM_DH), st3),
            pl.BlockSpec((1, MLSTM_H, LANES), st3),
        ],
        out_specs=[
            pl.BlockSpec((length, MLSTM_W), lambda b, c: (b * n_chunks + c, 0)),
            pl.BlockSpec((1, MLSTM_H, MLSTM_DH, MLSTM_DH), st4),
            pl.BlockSpec((1, MLSTM_H, MLSTM_DH), st3),
            pl.BlockSpec((1, MLSTM_H, LANES), st3),
        ],
        out_shape=[
            jax.ShapeDtypeStruct((rows, MLSTM_W), BF16),
            jax.ShapeDtypeStruct((n_seq, MLSTM_H, MLSTM_DH, MLSTM_DH), F32),
            jax.ShapeDtypeStruct((n_seq, MLSTM_H, MLSTM_DH), F32),
            jax.ShapeDtypeStruct((n_seq, MLSTM_H, LANES), F32),
        ],
        scratch_shapes=[pltpu.VMEM((MLSTM_H, MLSTM_DH, 2 * MLSTM_DH), F32),
                        pltpu.VMEM((MLSTM_H, LANES), F32)],
        compiler_params=pltpu.CompilerParams(
            dimension_semantics=("parallel", "arbitrary"), vmem_limit_bytes=VMEM_LIMIT),
        name=name,
    )(pa, pa, pa, pa, pa, pc, bias, norm, c0, n0, m0)


def _hgrn_kernel(q_ref, f_ref, i_ref, z_ref, lb_ref, norm_ref, s0_ref, y_ref, s_out,
                 st_scr, sc_scr, *, length):
    c_idx = pl.program_id(1)
    dk = HGRN_DK
    sub = min(HGRN_SUB, length)
    n_sub = length // sub

    @pl.when(c_idx == 0)
    def _():
        for h in range(HGRN_H):
            st_scr[h] = s0_ref[0, h].T

    lb = lb_ref[...]
    fp = f_ref[...]
    f_gate = lb + (1.0 - lb) * _sigmoid(fp)
    k_all = (1.0 - lb) * _sigmoid(-fp)
    tri_b = _bf(jnp.where(_tri(length), 1.0, 0.0))
    a_all = _cumsum_time(jnp.log(f_gate), tri_b)
    sub_causal = _tri(sub)
    lane_sub = lax.broadcasted_iota(jnp.int32, (sub, sub), 1)

    for h in range(HGRN_H):
        sl = slice(h * dk, (h + 1) * dk)
        q = q_ref[:, sl]
        k = k_all[:, sl]
        v = i_ref[:, sl]
        a = a_all[:, sl]
        vb = _bf(v)
        st = st_scr[h]

        sc_scr[...] = jnp.zeros_like(sc_scr)
        for i in range(n_sub):
            r0 = i * sub
            q_i = q[r0:r0 + sub]
            a_i = a[r0:r0 + sub]
            k_i = k[r0:r0 + sub]
            if i > 0:
                ref_row = a[r0:r0 + 1]
                q_s = _bf(q_i * jnp.exp(a_i - ref_row))
                k_s = _bf(k[0:r0] * jnp.exp(ref_row - a[0:r0]))
                sc_scr[r0:r0 + sub, 0:r0] = _dot_nt(q_s, k_s)
            blk = jnp.zeros((sub, sub), F32)
            for s_idx in range(sub):
                e = jnp.exp(jnp.minimum(a_i - a_i[s_idx:s_idx + 1], 0.0))
                col_v = jnp.sum(q_i * e * k_i[s_idx:s_idx + 1], axis=-1, keepdims=True)
                blk = jnp.where(lane_sub == s_idx, col_v, blk)
            sc_scr[r0:r0 + sub, r0:r0 + sub] = jnp.where(sub_causal, blk, 0.0)

        o = _dot_nt(_bf(q * jnp.exp(a)), _bf(st)) + _dot(_bf(sc_scr[0:length, 0:length]), vb)
        o = o * lax.rsqrt(jnp.mean(o * o, axis=-1, keepdims=True) + EPS)
        y_ref[:, sl] = _bf(o * norm_ref[:, sl] * _silu(z_ref[:, sl]))

        a_end = a[length - 1:length]
        st_scr[h] = jnp.exp(a_end) * st + _dot_tn(vb, _bf(k * jnp.exp(a_end - a)))

    @pl.when(c_idx == pl.num_programs(1) - 1)
    def _():
        for h in range(HGRN_H):
            s_out[0, h] = st_scr[h].T


def _hgrn_call(pb, lb, norm, s0, *, n_seq, n_chunks, length, row_off, name):
    assert row_off % length == 0
    blk0 = row_off // length
    rows = n_seq * n_chunks * length

    def col(j):
        return lambda b, c: (blk0 + b * n_chunks + c, j)

    st4 = lambda b, c: (b, 0, 0, 0)
    return pl.pallas_call(
        functools.partial(_hgrn_kernel, length=length),
        grid=(n_seq, n_chunks),
        in_specs=[
            pl.BlockSpec((length, HGRN_W), col(0)),
            pl.BlockSpec((length, HGRN_W), col(1)),
            pl.BlockSpec((length, HGRN_W), col(2)),
            pl.BlockSpec((length, HGRN_W), col(3)),
            pl.BlockSpec((1, HGRN_W), lambda b, c: (0, 0)),
            pl.BlockSpec((1, HGRN_W), lambda b, c: (0, 0)),
            pl.BlockSpec((1, HGRN_H, HGRN_DK, HGRN_DV), st4),
        ],
        out_specs=[
            pl.BlockSpec((length, HGRN_W), lambda b, c: (b * n_chunks + c, 0)),
            pl.BlockSpec((1, HGRN_H, HGRN_DK, HGRN_DV), st4),
        ],
        out_shape=[
            jax.ShapeDtypeStruct((rows, HGRN_W), BF16),
            jax.ShapeDtypeStruct((n_seq, HGRN_H, HGRN_DK, HGRN_DV), F32),
        ],
        scratch_shapes=[pltpu.VMEM((HGRN_H, HGRN_DV, HGRN_DK), F32),
                        pltpu.VMEM((CHUNK, CHUNK), F32)],
        compiler_params=pltpu.CompilerParams(
            dimension_semantics=("parallel", "arbitrary"), vmem_limit_bytes=VMEM_LIMIT),
        name=name,
    )(pb, pb, pb, pb, lb, norm, s0)


def _rwkv_kernel(pc_ref, mu_ref, wa_up_ref, w0_ref, a0_ref, kk_ref, ka_ref, rk_ref, gg_ref, gb_ref,
                 s0_ref, shift0_ref, y_ref, s_out, shift_out, s_scr, carry_scr, *, length):
    c_idx = pl.program_id(1)
    dh = RWKV_DH
    w = RWKV_W

    @pl.when(c_idx == 0)
    def _():
        s_scr[...] = s0_ref[0]
        carry_scr[...] = shift0_ref[0]

    pc = pc_ref[...]
    row0 = lax.broadcasted_iota(jnp.int32, (length, C_COLS), 0) == 0
    prev = jnp.where(row0, carry_scr[...], pltpu.roll(pc, 1, 0))
    carry_scr[...] = pc[length - 1:length, :]
    xs = pc + mu_ref[...] * (prev - pc)
    r = xs[:, 0:w]
    k = xs[:, w:2 * w]
    v = xs[:, 2 * w:3 * w]
    low = xs[:, 3 * w:3 * w + 2 * RWKV_RANK]
    z = xs[:, 3 * w + 2 * RWKV_RANK:]
    lane = lax.broadcasted_iota(jnp.int32, (length, 2 * RWKV_RANK), 1)
    low = jnp.where(lane < RWKV_RANK, jnp.tanh(low), low)
    up = _dot(_bf(low), wa_up_ref[...])
    w_logit = -_softplus(-(w0_ref[...] + up[:, 0:w])) - 0.5
    log_w = -jnp.exp(w_logit)
    a = _sigmoid(a0_ref[...] + up[:, w:2 * w])
    kk_raw = k * kk_ref[...]
    k2 = k * (1.0 + (a - 1.0) * ka_ref[...])
    rk_bonus = r * k2 * rk_ref[...]
    tri_b = _bf(jnp.where(_tri(length), 1.0, 0.0))
    cum = _cumsum_time(log_w, tri_b)
    strict = _tri(length, strict=True)
    incl = _tri(length)
    n_double = int(math.log2(length))
    assert 2 ** n_double == length

    for h in range(RWKV_H):
        sl = slice(h * dh, (h + 1) * dh)
        cum_h = cum[:, sl]
        r_h = r[:, sl]
        v_h = v[:, sl]
        a_h = a[:, sl]
        k2_h = k2[:, sl]
        kk_h = kk_raw[:, sl]
        kk_h = kk_h / jnp.maximum(jnp.sqrt(jnp.sum(kk_h * kk_h, axis=-1, keepdims=True)), 1e-12)
        p_in = jnp.exp(cum_h)
        p_ex = jnp.exp(cum_h - log_w[:, sl])
        p_inv = jnp.exp(-cum_h)
        lhs = _bf(jnp.concatenate([-kk_h * p_ex, r_h * p_in], axis=0))
        rhs = _bf(jnp.concatenate([kk_h * a_h * p_inv, k2_h * p_inv], axis=0))
        gram = _dot_nt(lhs, rhs)
        n_mat = jnp.where(strict, gram[0:length, 0:length], 0.0)
        m_mat = jnp.where(strict, gram[0:length, length:], 0.0)
        ab_mat = jnp.where(incl, gram[length:, 0:length], 0.0)
        ak_mat = jnp.where(incl, gram[length:, length:], 0.0)
        s_old = s_scr[h]
        uy0 = _dot_nt(lhs, _bf(s_old))
        vb = _bf(v_h)
        u = uy0[0:length] + _dot(_bf(m_mat), vb)
        n_pow = n_mat
        for step in range(n_double):
            n_b = _bf(n_pow)
            u = u + _dot(n_b, _bf(u))
            if step + 1 < n_double:
                n_pow = _dot(n_b, n_b)
        ub = _bf(u)
        y = uy0[length:] + _dot(_bf(ab_mat), ub) + _dot(_bf(ak_mat), vb)
        cum_end = cum_h[length - 1:length]
        p_end = jnp.exp(cum_end - cum_h)
        s_scr[h] = (s_old * jnp.exp(cum_end) + _dot_tn(ub, _bf(kk_h * a_h * p_end))
                    + _dot_tn(vb, _bf(k2_h * p_end)))

        mu = jnp.mean(y, axis=-1, keepdims=True)
        cen = y - mu
        var = jnp.mean(cen * cen, axis=-1, keepdims=True)
        y = cen * lax.rsqrt(var + RWKV_GN_EPS) * gg_ref[:, sl] + gb_ref[:, sl]
        bonus = jnp.sum(rk_bonus[:, sl], axis=-1, keepdims=True) * v_h
        y_ref[:, sl] = _bf((y + bonus) * _silu(z[:, sl]))

    @pl.when(c_idx == pl.num_programs(1) - 1)
    def _():
        s_out[0] = s_scr[...]
        shift_out[0] = carry_scr[...]


def _rwkv_call(pc, p, s0, shift0, *, n_seq, n_chunks, length, row_off, name):
    assert row_off % length == 0
    blk0 = row_off // length
    rows = n_seq * n_chunks * length
    vec = pl.BlockSpec((1, RWKV_W), lambda b, c: (0, 0))
    st4 = lambda b, c: (b, 0, 0, 0)
    st3 = lambda b, c: (b, 0, 0)
    return pl.pallas_call(
        functools.partial(_rwkv_kernel, length=length),
        grid=(n_seq, n_chunks),
        in_specs=[
            pl.BlockSpec((length, C_COLS), lambda b, c: (blk0 + b * n_chunks + c, 0)),
            pl.BlockSpec((1, C_COLS), lambda b, c: (0, 0)),
            pl.BlockSpec((2 * RWKV_RANK, 2 * RWKV_W), lambda b, c: (0, 0)),
            vec, vec, vec, vec, vec, vec, vec,
            pl.BlockSpec((1, RWKV_H, RWKV_DH, RWKV_DH), st4),
            pl.BlockSpec((1, 1, C_COLS), st3),
        ],
        out_specs=[
            pl.BlockSpec((length, RWKV_W), lambda b, c: (b * n_chunks + c, 0)),
            pl.BlockSpec((1, RWKV_H, RWKV_DH, RWKV_DH), st4),
            pl.BlockSpec((1, 1, C_COLS), st3),
        ],
        out_shape=[
            jax.ShapeDtypeStruct((rows, RWKV_W), BF16),
            jax.ShapeDtypeStruct((n_seq, RWKV_H, RWKV_DH, RWKV_DH), F32),
            jax.ShapeDtypeStruct((n_seq, 1, C_COLS), F32),
        ],
        scratch_shapes=[pltpu.VMEM((RWKV_H, RWKV_DH, RWKV_DH), F32),
                        pltpu.VMEM((1, C_COLS), F32)],
        compiler_params=pltpu.CompilerParams(
            dimension_semantics=("parallel", "arbitrary"), vmem_limit_bytes=VMEM_LIMIT),
        name=name,
    )(pc, p['mu'], p['wa_up'], p['w0'], p['a0'], p['k_k'], p['k_a'], p['r_k'], p['gn_g'], p['gn_b'],
      s0, shift0)


def _prep_layer(l, lb_all, norm_pre, norm_post, w_in, mlstm_b_i, mlstm_b_f, mlstm_norm, hgrn_norm,
                rwkv_mu, rwkv_w0, rwkv_w_up, rwkv_a0, rwkv_a_up, rwkv_k_k, rwkv_k_a, rwkv_r_k,
                rwkv_gn_g, rwkv_gn_b, w_proj_a, w_proj_b, w_proj_c, w_out):
    w = w_in[l]
    a_cols = 5 * MLSTM_W + 2 * MLSTM_H
    if0 = 3 * MLSTM_W
    b0 = a_cols
    c0 = b0 + B_W
    g0 = c0 + C_COLS
    w_a = _bf(jnp.concatenate([w[:, 0:if0], w[:, if0 + 2 * MLSTM_H:a_cols]], axis=1))
    w_b = _bf(w[:, b0:c0])
    w_c = _bf(jnp.concatenate(
        [w[:, c0:g0], w[:, if0:if0 + 2 * MLSTM_H],
         jnp.zeros((D_MODEL, LANES - 2 * MLSTM_H), F32)], axis=1))
    n_chunk = D_MODEL // MERGE_NC
    w_g = _bf(w[:, g0:].reshape(D_MODEL, 3, n_chunk, MERGE_NC).transpose(0, 2, 1, 3)
              .reshape(D_MODEL, 3 * D_MODEL))
    bias = jnp.concatenate([mlstm_b_i[l], mlstm_b_f[l],
                            jnp.zeros((LANES - 2 * MLSTM_H,), F32)]).reshape(1, LANES)
    zero = jnp.zeros((RWKV_RANK, RWKV_W), F32)
    wa_up = _bf(jnp.concatenate([jnp.concatenate([rwkv_w_up[l], zero], axis=1),
                                 jnp.concatenate([zero, rwkv_a_up[l]], axis=1)], axis=0))
    row = lambda a: a.reshape(1, -1).astype(F32)
    return dict(
        norm_pre=row(norm_pre[l]), norm_post=row(norm_post[l]), w_a=w_a, w_b=w_b, w_c=w_c, w_g=w_g,
        bias=bias, mlstm_norm=row(mlstm_norm[l]), lb=row(lb_all[l]), hgrn_norm=row(hgrn_norm[l]),
        rwkv=dict(mu=row(rwkv_mu[l]), wa_up=wa_up, w0=row(rwkv_w0[l]), a0=row(rwkv_a0[l]),
                  k_k=row(rwkv_k_k[l]), k_a=row(rwkv_k_a[l]), r_k=row(rwkv_r_k[l]),
                  gn_g=row(rwkv_gn_g[l]), gn_b=row(rwkv_gn_b[l])),
        wpa=_bf(w_proj_a[l]), wpb=_bf(w_proj_b[l]), wpc=_bf(w_proj_c[l]), wo=_bf(w_out[l]))


def _lane_bcast(m):
    return jnp.broadcast_to(m[..., None], m.shape + (LANES,))


def _branches(p, proj_main, proj_tail, st_s, *, bp, t_p, bs, t_s, tag):
    pa_m, pb_m, pc_m = proj_main
    pa_t, pb_t, pc_t = proj_tail
    c_s, n_s, m_s, sh_s, sr_s, shift_s = st_s
    n_chunks = t_p // CHUNK
    meta_rows = bp * N_META
    zeros = lambda *s: jnp.zeros(s, F32)

    ya_meta, c, n, m = _mlstm_call(
        pa_t, pc_t, p['bias'], p['mlstm_norm'], zeros(bp, MLSTM_H, MLSTM_DH, MLSTM_DH),
        zeros(bp, MLSTM_H, MLSTM_DH), zeros(bp, MLSTM_H, LANES),
        n_seq=bp, n_chunks=1, length=N_META, row_off=0, name=f'mlstm_meta{tag}')
    ya_main, c_p, n_p, m_p = _mlstm_call(
        pa_m, pc_m, p['bias'], p['mlstm_norm'], c, n, m,
        n_seq=bp, n_chunks=n_chunks, length=CHUNK, row_off=0, name=f'mlstm_main{tag}')
    ya_samp, c_so, n_so, m_so = _mlstm_call(
        pa_t, pc_t, p['bias'], p['mlstm_norm'], c_s, n_s, _lane_bcast(m_s),
        n_seq=bs, n_chunks=1, length=t_s, row_off=meta_rows, name=f'mlstm_samp{tag}')

    yb_meta, s = _hgrn_call(pb_t, p['lb'], p['hgrn_norm'], zeros(bp, HGRN_H, HGRN_DK, HGRN_DV),
                            n_seq=bp, n_chunks=1, length=N_META, row_off=0, name=f'hgrn_meta{tag}')
    yb_main, sh_p = _hgrn_call(pb_m, p['lb'], p['hgrn_norm'], s,
                               n_seq=bp, n_chunks=n_chunks, length=CHUNK, row_off=0,
                               name=f'hgrn_main{tag}')
    yb_samp, sh_so = _hgrn_call(pb_t, p['lb'], p['hgrn_norm'], sh_s,
                                n_seq=bs, n_chunks=1, length=t_s, row_off=meta_rows,
                                name=f'hgrn_samp{tag}')

    yc_meta, s, shift = _rwkv_call(pc_t, p['rwkv'], zeros(bp, RWKV_H, RWKV_DH, RWKV_DH),
                                   zeros(bp, 1, C_COLS), n_seq=bp, n_chunks=1, length=N_META,
                                   row_off=0, name=f'rwkv_meta{tag}')
    yc_main, sr_p, shift_p = _rwkv_call(pc_m, p['rwkv'], s, shift, n_seq=bp, n_chunks=n_chunks,
                                        length=CHUNK, row_off=0, name=f'rwkv_main{tag}')
    yc_samp, sr_so, shift_so = _rwkv_call(pc_t, p['rwkv'], sr_s, shift_s, n_seq=bs, n_chunks=1,
                                          length=t_s, row_off=meta_rows, name=f'rwkv_samp{tag}')

    y_main = (ya_main, yb_main, yc_main)
    y_tail = tuple(jnp.concatenate([a, b], axis=0)
                   for a, b in ((ya_meta, ya_samp), (yb_meta, yb_samp), (yc_meta, yc_samp)))
    st_p_out = (c_p, n_p, m_p[:, :, 0], sh_p, sr_p, shift_p)
    st_s_out = (c_so, n_so, m_so[:, :, 0], sh_so, sr_so, shift_so)
    return y_main, y_tail, st_p_out, st_s_out


def kernel(x_prompt, x_sample, state_mlstm_C, state_mlstm_n, state_mlstm_m, state_hgrn_S,
           state_rwkv_S, cache_rwkv_shift, meta_tokens, norm_pre, norm_post, w_in,
           mlstm_b_i, mlstm_b_f, mlstm_norm, hgrn_lb_logits, hgrn_norm, rwkv_mu, rwkv_w0,
           rwkv_w_up, rwkv_a0, rwkv_a_up, rwkv_k_k, rwkv_k_a, rwkv_r_k, rwkv_gn_g, rwkv_gn_b,
           w_proj_a, w_proj_b, w_proj_c, w_out):
    bp, t_p, _ = x_prompt.shape
    bs, t_s, _ = x_sample.shape
    depth = w_in.shape[0]
    assert t_p % CHUNK == 0 and t_s % HGRN_SUB == 0 and (bp * N_META) % t_s == 0

    sm = jax.nn.softmax(hgrn_lb_logits.astype(F32), axis=0)
    lb_all = jnp.cumsum(sm, axis=0) - sm[0]

    x_main = x_prompt.reshape(bp * t_p, D_MODEL)
    meta = jnp.broadcast_to(meta_tokens.astype(F32)[None], (bp, N_META, D_MODEL))
    x_tail = jnp.concatenate([meta.reshape(bp * N_META, D_MODEL),
                              x_sample.reshape(bs * t_s, D_MODEL)], axis=0)

    outs_p, outs_s = [], []
    for l in range(depth):
        p = _prep_layer(l, lb_all, norm_pre, norm_post, w_in, mlstm_b_i, mlstm_b_f, mlstm_norm,
                        hgrn_norm, rwkv_mu, rwkv_w0, rwkv_w_up, rwkv_a0, rwkv_a_up, rwkv_k_k,
                        rwkv_k_a, rwkv_r_k, rwkv_gn_g, rwkv_gn_b, w_proj_a, w_proj_b, w_proj_c,
                        w_out)
        proj_main = tuple(_proj(x_main, p['norm_pre'], p[k], f'proj_{k}_main{l}')
                          for k in ('w_a', 'w_b', 'w_c'))
        proj_tail = tuple(_proj(x_tail, p['norm_pre'], p[k], f'proj_{k}_tail{l}')
                          for k in ('w_a', 'w_b', 'w_c'))
        st_s = (state_mlstm_C[l], state_mlstm_n[l], state_mlstm_m[l], state_hgrn_S[l],
                state_rwkv_S[l], cache_rwkv_shift[l])
        y_main, y_tail, st_p_out, st_s_out = _branches(
            p, proj_main, proj_tail, st_s, bp=bp, t_p=t_p, bs=bs, t_s=t_s, tag=str(l))
        x_main = _merge(x_main, p['norm_pre'], *y_main, p['w_g'], p['wpa'], p['wpb'], p['wpc'],
                        p['wo'], p['norm_post'], f'merge_main{l}')
        x_tail = _merge(x_tail, p['norm_pre'], *y_tail, p['w_g'], p['wpa'], p['wpb'], p['wpc'],
                        p['wo'], p['norm_post'], f'merge_tail{l}')
        outs_p.append(st_p_out)
        outs_s.append(st_s_out)

    states_p = tuple(jnp.stack([o[j] for o in outs_p]) for j in range(6))
    states_s = tuple(jnp.stack([o[j] for o in outs_s]) for j in range(6))
    y_prompt = x_main.reshape(bp, t_p, D_MODEL)
    y_sample = x_tail[bp * N_META:].reshape(bs, t_s, D_MODEL)
    return (y_prompt, y_sample) + states_p + states_s
```

```python
import functools
import math

import jax
import jax.numpy as jnp
from jax import lax
from jax.experimental import pallas as pl
from jax.experimental.pallas import tpu as pltpu

F32 = jnp.float32
BF16 = jnp.bfloat16

D_MODEL = 2048
CHUNK = 64
N_META = 16
EPS = 1e-6

MLSTM_H = 8
MLSTM_DH = 128
MLSTM_W = MLSTM_H * MLSTM_DH
HGRN_H = 4
HGRN_DK = 128
HGRN_DV = 128
HGRN_W = HGRN_H * HGRN_DV
RWKV_H = 8
RWKV_DH = 64
RWKV_W = RWKV_H * RWKV_DH
RWKV_RANK = 64
RWKV_GN_EPS = 64e-5
C_COLS = 4 * RWKV_W + 2 * RWKV_RANK

LANES = 128
HGRN_SUB = 16
A_W = 5 * MLSTM_W
B_W = 4 * HGRN_W
C_W = C_COLS + LANES
IF_BLOCK = C_COLS // LANES
MERGE_NC = 256
VMEM_LIMIT = 56 * 1024 * 1024


def _dot(a, b):
    return jnp.dot(a, b, preferred_element_type=F32)


def _dot_nt(a, b):
    return lax.dot_general(a, b, (((1,), (1,)), ((), ())), preferred_element_type=F32)


def _dot_tn(a, b):
    return lax.dot_general(a, b, (((0,), (0,)), ((), ())), preferred_element_type=F32)


def _bf(a):
    return a.astype(BF16)


def _sigmoid(x):
    return 1.0 / (1.0 + jnp.exp(-x))


def _silu(x):
    return x * _sigmoid(x)


def _softplus(x):
    return jnp.maximum(x, 0.0) + jnp.log1p(jnp.exp(-jnp.abs(x)))


def _tri(length, strict=False):
    row = lax.broadcasted_iota(jnp.int32, (length, length), 0)
    col = lax.broadcasted_iota(jnp.int32, (length, length), 1)
    return (row > col) if strict else (row >= col)


def _cumsum_time(x, tri_b):
    hi = _bf(x)
    r1 = x - hi.astype(F32)
    mid = _bf(r1)
    lo = _bf(r1 - mid.astype(F32))
    return _dot(tri_b, hi) + _dot(tri_b, mid) + _dot(tri_b, lo)


def _rmsnorm(x, g):
    return x * lax.rsqrt(jnp.mean(x * x, axis=-1, keepdims=True) + EPS) * g


def _row_tile(rows, cap):
    best = None
    for t in range(16, min(rows, cap) + 1, 16):
        if rows % t == 0:
            best = t
    assert best is not None, rows
    return best


def _col_tile(cols, cap):
    best = None
    for t in range(LANES, min(cols, cap) + 1, LANES):
        if cols % t == 0:
            best = t
    assert best is not None, cols
    return best


def _proj_kernel(x_ref, g_ref, w_ref, o_ref):
    h = _bf(_rmsnorm(x_ref[...], g_ref[...]))
    o_ref[...] = _dot(h, w_ref[...])


def _proj(x, g, w, name):
    rows, _ = x.shape
    cols = w.shape[1]
    tm = _row_tile(rows, 1024)
    tn = _col_tile(cols, 1280)
    return pl.pallas_call(
        _proj_kernel,
        grid=(cols // tn, rows // tm),
        in_specs=[
            pl.BlockSpec((tm, D_MODEL), lambda j, i: (i, 0)),
            pl.BlockSpec((1, D_MODEL), lambda j, i: (0, 0)),
            pl.BlockSpec((D_MODEL, tn), lambda j, i: (0, j)),
        ],
        out_specs=pl.BlockSpec((tm, tn), lambda j, i: (i, j)),
        out_shape=jax.ShapeDtypeStruct((rows, cols), F32),
        compiler_params=pltpu.CompilerParams(
            dimension_semantics=("parallel", "parallel"), vmem_limit_bytes=VMEM_LIMIT),
        name=name,
    )(x, g, w)


def _merge_kernel(x_ref, gpre_ref, ya_ref, yb_ref, yc_ref, wga_ref, wgb_ref, wgc_ref,
                  wpa_ref, wpb_ref, wpc_ref, wo_ref, gpost_ref, o_ref, h_scr, acc_scr):
    j = pl.program_id(1)

    @pl.when(j == 0)
    def _():
        h_scr[...] = _bf(_rmsnorm(x_ref[...], gpre_ref[...]))
        acc_scr[...] = jnp.zeros_like(acc_scr)

    h = h_scr[...]
    merged = (_sigmoid(_dot(h, wga_ref[...])) * _dot(ya_ref[...], wpa_ref[...])
              + _sigmoid(_dot(h, wgb_ref[...])) * _dot(yb_ref[...], wpb_ref[...])
              + _sigmoid(_dot(h, wgc_ref[...])) * _dot(yc_ref[...], wpc_ref[...]))
    acc_scr[...] += _dot(_bf(merged), wo_ref[...])

    @pl.when(j == pl.num_programs(1) - 1)
    def _():
        o_ref[...] = x_ref[...] + _rmsnorm(acc_scr[...], gpost_ref[...])


def _merge(x, gpre, ya, yb, yc, wg, wpa, wpb, wpc, wo, gpost, name):
    rows = x.shape[0]
    tm = _row_tile(rows, 640)
    nc = MERGE_NC
    n_chunk = D_MODEL // nc
    row = lambda i, j: (i, 0)

    def gate(b):
        return pl.BlockSpec((D_MODEL, nc), lambda i, j: (0, b * n_chunk + j))

    return pl.pallas_call(
        _merge_kernel,
        grid=(rows // tm, n_chunk),
        in_specs=[
            pl.BlockSpec((tm, D_MODEL), row),
            pl.BlockSpec((1, D_MODEL), lambda i, j: (0, 0)),
            pl.BlockSpec((tm, MLSTM_W), row),
            pl.BlockSpec((tm, HGRN_W), row),
            pl.BlockSpec((tm, RWKV_W), row),
            gate(0), gate(1), gate(2),
            pl.BlockSpec((MLSTM_W, nc), lambda i, j: (0, j)),
            pl.BlockSpec((HGRN_W, nc), lambda i, j: (0, j)),
            pl.BlockSpec((RWKV_W, nc), lambda i, j: (0, j)),
            pl.BlockSpec((nc, D_MODEL), lambda i, j: (j, 0)),
            pl.BlockSpec((1, D_MODEL), lambda i, j: (0, 0)),
        ],
        out_specs=pl.BlockSpec((tm, D_MODEL), row),
        out_shape=jax.ShapeDtypeStruct((rows, D_MODEL), F32),
        scratch_shapes=[pltpu.VMEM((tm, D_MODEL), BF16), pltpu.VMEM((tm, D_MODEL), F32)],
        compiler_params=pltpu.CompilerParams(
            dimension_semantics=("parallel", "arbitrary"), vmem_limit_bytes=VMEM_LIMIT),
        name=name,
    )(x, gpre, ya, yb, yc, wg, wg, wg, wpa, wpb, wpc, wo, gpost)


def _mlstm_kernel(q_ref, k_ref, v_ref, o_ref, z_ref, if_ref, bias_ref, norm_ref,
                  c0_ref, n0_ref, m0_ref, y_ref, c_out, n_out, m_out, cn_scr, m_scr, *, length):
    c_idx = pl.program_id(1)
    dh = MLSTM_DH
    lane0 = lax.broadcasted_iota(jnp.int32, (dh, dh), 1) == 0

    @pl.when(c_idx == 0)
    def _():
        n_t = n0_ref[0].T
        for h in range(MLSTM_H):
            cn_scr[h, :, 0:dh] = c0_ref[0, h].T
            cn_scr[h, :, dh:2 * dh] = jnp.where(lane0, n_t[:, h:h + 1], 0.0)
        m_scr[...] = m0_ref[0]

    cns = [cn_scr[h] for h in range(MLSTM_H)]
    m_all = m_scr[...]

    causal = _tri(length)
    tri_b = _bf(jnp.where(causal, 1.0, 0.0))
    gates = if_ref[...] + bias_ref[...]
    log_f = jnp.minimum(gates, 0.0) - jnp.log1p(jnp.exp(-jnp.abs(gates)))
    b_all = _cumsum_time(log_f, tri_b)
    gates_t = gates.T
    b_all_t = b_all.T
    ones_blk = jnp.where(lax.broadcasted_iota(jnp.int32, (length, dh), 1) == 0, 1.0, 0.0)
    scale = MLSTM_DH ** -0.5

    new_cn, new_m = [], []
    for h in range(MLSTM_H):
        sl = slice(h * dh, (h + 1) * dh)
        q = q_ref[:, sl]
        k = k_ref[:, sl] * scale
        v = v_ref[:, sl]
        b_col = b_all[:, MLSTM_H + h:MLSTM_H + h + 1]
        ig_col = gates[:, h:h + 1]
        b_row = b_all_t[MLSTM_H + h:MLSTM_H + h + 1, :]
        ig_row = gates_t[h:h + 1, :]
        m_prev = m_all[h:h + 1, 0:1]

        d = jnp.where(causal, b_col - b_row + ig_row, -jnp.inf)
        inter = b_col + m_prev
        m_t = jnp.maximum(inter, jnp.max(d, axis=-1, keepdims=True))
        w_inter = jnp.exp(inter - m_t)
        qb = _bf(q)
        s = _dot_nt(qb, _bf(k)) * jnp.exp(d - m_t)
        cn = cns[h]
        v1 = _bf(jnp.concatenate([v, ones_blk], axis=1))
        tot = w_inter * _dot(qb, _bf(cn)) + _dot(_bf(s), v1)
        num = tot[:, 0:dh]
        den = tot[:, dh:dh + 1]
        hid = num / jnp.maximum(jnp.abs(den), jnp.exp(-m_t))
        mu = jnp.mean(hid, axis=-1, keepdims=True)
        cen = hid - mu
        var = jnp.mean(cen * cen, axis=-1, keepdims=True)
        hid = cen * lax.rsqrt(var + EPS) * norm_ref[:, sl]
        y_ref[:, sl] = _bf(hid * _sigmoid(o_ref[:, sl]) * _silu(z_ref[:, sl]))

        b_end = b_col[length - 1:length, :]
        g_col = b_end - b_col + ig_col
        m_new = jnp.maximum(b_end + m_prev, jnp.max(g_col, axis=0, keepdims=True))
        w_old = jnp.exp(b_end + m_prev - m_new)
        w_s = jnp.exp(g_col - m_new)
        new_cn.append(w_old * cn + _dot_tn(_bf(k * w_s), v1))
        new_m.append(jnp.broadcast_to(m_new, (1, LANES)))

    for h in range(MLSTM_H):
        cn_scr[h] = new_cn[h]
    m_scr[...] = jnp.concatenate(new_m, axis=0)

    @pl.when(c_idx == pl.num_programs(1) - 1)
    def _():
        for h in range(MLSTM_H):
            c_out[0, h] = cn_scr[h, :, 0:dh].T
            n_out[0, h:h + 1, :] = cn_scr[h, :, dh:2 * dh].T[0:1, :]
        m_out[0] = m_scr[...]


def _mlstm_call(pa, pc, bias, norm, c0, n0, m0, *, n_seq, n_chunks, length, row_off, name):
    assert row_off % length == 0
    blk0 = row_off // length
    rows = n_seq * n_chunks * length

    def col(j):
        return lambda b, c: (blk0 + b * n_chunks + c, j)

    st4 = lambda b, c: (b, 0, 0, 0)
    st3 = lambda b, c: (b, 0, 0)
    return pl.pallas_call(
        functools.partial(_mlstm_kernel, length=length),
        grid=(n_seq, n_chunks),
        in_specs=[
            pl.BlockSpec((length, MLSTM_W), col(0)),
            pl.BlockSpec((length, MLSTM_W), col(1)),
            pl.BlockSpec((length, MLSTM_W), col(2)),
            pl.BlockSpec((length, MLSTM_W), col(3)),
            pl.BlockSpec((length, MLSTM_W), col(4)),
            pl.BlockSpec((length, LANES), col(IF_BLOCK)),
            pl.BlockSpec((1, LANES), lambda b, c: (0, 0)),
            pl.BlockSpec((1, MLSTM_W), lambda b, c: (0, 0)),
            pl.BlockSpec((1, MLSTM_H, MLSTM_DH, MLSTM_DH), st4),
            pl.BlockSpec((1, MLSTM_H, MLSTM_DH), st3),
            pl.BlockSpec((1, MLSTM_H, LANES), st3),
        ],
        out_specs=[
            pl.BlockSpec((length, MLSTM_W), lambda b, c: (b * n_chunks + c, 0)),
            pl.BlockSpec((1, MLSTM_H, MLSTM_DH, MLSTM_DH), st4),
            pl.BlockSpec((1, MLSTM_H, MLSTM_DH), st3),
            pl.BlockSpec((1, MLSTM_H, LANES), st3),
        ],
        out_shape=[
            jax.ShapeDtypeStruct((rows, MLSTM_W), BF16),
            jax.ShapeDtypeStruct((n_seq, MLSTM_H, MLSTM_DH, MLSTM_DH), F32),
            jax.ShapeDtypeStruct((n_seq, MLSTM_H, MLSTM_DH), F32),
            jax.ShapeDtypeStruct((n_seq, MLSTM_H, LANES), F32),
        ],
        scratch_shapes=[pltpu.VMEM((MLSTM_H, MLSTM_DH, 2 * MLSTM_DH), F32),
                        pltpu.VMEM((MLSTM_H, LANES), F32)],
        compiler_params=pltpu.CompilerParams(
            dimension_semantics=("parallel", "arbitrary"), vmem_limit_bytes=VMEM_LIMIT),
        name=name,
    )(pa, pa, pa, pa, pa, pc, bias, norm, c0, n0, m0)


def _hgrn_kernel(q_ref, f_ref, i_ref, z_ref, lb_ref, norm_ref, s0_ref, y_ref, s_out,
                 st_scr, *, length):
    c_idx = pl.program_id(1)
    dk = HGRN_DK
    sub = min(HGRN_SUB, length)
    n_sub = length // sub

    @pl.when(c_idx == 0)
    def _():
        for h in range(HGRN_H):
            st_scr[h] = s0_ref[0, h].T

    sts = [st_scr[h] for h in range(HGRN_H)]

    lb = lb_ref[...]
    fp = f_ref[...]
    f_gate = lb + (1.0 - lb) * _sigmoid(fp)
    k_all = (1.0 - lb) * _sigmoid(-fp)
    tri_b = _bf(jnp.where(_tri(length), 1.0, 0.0))
    a_all = _cumsum_time(jnp.log(f_gate), tri_b)
    lane_l = lax.broadcasted_iota(jnp.int32, (sub, length), 1)
    row_l = lax.broadcasted_iota(jnp.int32, (sub, length), 0)

    new_st = []
    for h in range(HGRN_H):
        sl = slice(h * dk, (h + 1) * dk)
        q = q_ref[:, sl]
        k = k_all[:, sl]
        v = i_ref[:, sl]
        a = a_all[:, sl]
        vb = _bf(v)
        st = sts[h]

        blocks = []
        for i in range(n_sub):
            r0 = i * sub
            q_i = q[r0:r0 + sub]
            a_i = a[r0:r0 + sub]
            k_i = k[r0:r0 + sub]
            if i > 0:
                ref_row = a[r0:r0 + 1]
                q_s = _bf(q_i * jnp.exp(a_i - ref_row))
                k_s = jnp.concatenate(
                    [_bf(k[0:r0] * jnp.exp(ref_row - a[0:r0])),
                     jnp.zeros((length - r0, dk), BF16)], axis=0)
                blk = _dot_nt(q_s, k_s)
            else:
                blk = jnp.zeros((sub, length), F32)
            for s_idx in range(sub):
                e = jnp.exp(jnp.minimum(a_i - a_i[s_idx:s_idx + 1], 0.0))
                col_v = jnp.sum(q_i * e * k_i[s_idx:s_idx + 1], axis=-1, keepdims=True)
                blk = jnp.where(lane_l == r0 + s_idx, col_v, blk)
            blocks.append(jnp.where(lane_l <= r0 + row_l, blk, 0.0))
        scores = jnp.concatenate(blocks, axis=0) if n_sub > 1 else blocks[0]

        o = _dot_nt(_bf(q * jnp.exp(a)), _bf(st)) + _dot(_bf(scores), vb)
        o = o * lax.rsqrt(jnp.mean(o * o, axis=-1, keepdims=True) + EPS)
        y_ref[:, sl] = _bf(o * norm_ref[:, sl] * _silu(z_ref[:, sl]))

        a_end = a[length - 1:length]
        new_st.append(jnp.exp(a_end) * st + _dot_tn(vb, _bf(k * jnp.exp(a_end - a))))

    for h in range(HGRN_H):
        st_scr[h] = new_st[h]

    @pl.when(c_idx == pl.num_programs(1) - 1)
    def _():
        for h in range(HGRN_H):
            s_out[0, h] = st_scr[h].T


def _hgrn_call(pb, lb, norm, s0, *, n_seq, n_chunks, length, row_off, name):
    assert row_off % length == 0
    blk0 = row_off // length
    rows = n_seq * n_chunks * length

    def col(j):
        return lambda b, c: (blk0 + b * n_chunks + c, j)

    st4 = lambda b, c: (b, 0, 0, 0)
    return pl.pallas_call(
        functools.partial(_hgrn_kernel, length=length),
        grid=(n_seq, n_chunks),
        in_specs=[
            pl.BlockSpec((length, HGRN_W), col(0)),
            pl.BlockSpec((length, HGRN_W), col(1)),
            pl.BlockSpec((length, HGRN_W), col(2)),
            pl.BlockSpec((length, HGRN_W), col(3)),
            pl.BlockSpec((1, HGRN_W), lambda b, c: (0, 0)),
            pl.BlockSpec((1, HGRN_W), lambda b, c: (0, 0)),
            pl.BlockSpec((1, HGRN_H, HGRN_DK, HGRN_DV), st4),
        ],
        out_specs=[
            pl.BlockSpec((length, HGRN_W), lambda b, c: (b * n_chunks + c, 0)),
            pl.BlockSpec((1, HGRN_H, HGRN_DK, HGRN_DV), st4),
        ],
        out_shape=[
            jax.ShapeDtypeStruct((rows, HGRN_W), BF16),
            jax.ShapeDtypeStruct((n_seq, HGRN_H, HGRN_DK, HGRN_DV), F32),
        ],
        scratch_shapes=[pltpu.VMEM((HGRN_H, HGRN_DV, HGRN_DK), F32)],
        compiler_params=pltpu.CompilerParams(
            dimension_semantics=("parallel", "arbitrary"), vmem_limit_bytes=VMEM_LIMIT),
        name=name,
    )(pb, pb, pb, pb, lb, norm, s0)


def _pair_sum(x, even):
    s_even = jnp.sum(jnp.where(even, x, 0.0), axis=-1, keepdims=True)
    s_odd = jnp.sum(jnp.where(even, 0.0, x), axis=-1, keepdims=True)
    return jnp.where(even, s_even, s_odd)


def _rwkv_kernel(pc_ref, mu_ref, wa_up_ref, w0_ref, a0_ref, kk_ref, ka_ref, rk_ref, gg_ref, gb_ref,
                 s0_ref, shift0_ref, y_ref, s_out, shift_out, bd_scr, carry_scr, *, length):
    c_idx = pl.program_id(1)
    dh = RWKV_DH
    w = RWKV_W
    n_pair = RWKV_H // 2
    l2 = 2 * length

    @pl.when(c_idx == 0)
    def _():
        zero = jnp.zeros((dh, dh), F32)
        for p in range(n_pair):
            top = jnp.concatenate([s0_ref[0, 2 * p], zero], axis=1)
            bot = jnp.concatenate([zero, s0_ref[0, 2 * p + 1]], axis=1)
            bd_scr[p] = jnp.concatenate([top, bot], axis=0)
        carry_scr[...] = shift0_ref[0]

    bds = [bd_scr[p] for p in range(n_pair)]

    pc = pc_ref[...]
    row0 = lax.broadcasted_iota(jnp.int32, (length, C_COLS), 0) == 0
    prev = jnp.where(row0, carry_scr[...], pltpu.roll(pc, 1, 0))
    carry_scr[...] = pc[length - 1:length, :]
    xs = pc + mu_ref[...] * (prev - pc)
    r = xs[:, 0:w]
    k = xs[:, w:2 * w]
    v = xs[:, 2 * w:3 * w]
    low = xs[:, 3 * w:3 * w + 2 * RWKV_RANK]
    z = xs[:, 3 * w + 2 * RWKV_RANK:]
    lane = lax.broadcasted_iota(jnp.int32, (length, 2 * RWKV_RANK), 1)
    low = jnp.where(lane < RWKV_RANK, jnp.tanh(low), low)
    up = _dot(_bf(low), wa_up_ref[...])
    w_logit = -_softplus(-(w0_ref[...] + up[:, 0:w])) - 0.5
    log_w = -jnp.exp(w_logit)
    a = _sigmoid(a0_ref[...] + up[:, w:2 * w])
    kk_raw = k * kk_ref[...]
    k2 = k * (1.0 + (a - 1.0) * ka_ref[...])
    rk_bonus = r * k2 * rk_ref[...]
    tri_b = _bf(jnp.where(_tri(length), 1.0, 0.0))
    cum = _cumsum_time(log_w, tri_b)
    p_in = jnp.exp(cum)
    p_ex = jnp.exp(cum - log_w)
    p_inv = jnp.exp(-cum)
    cum_end = cum[length - 1:length]
    p_end = jnp.exp(cum_end - cum)
    dec_end = jnp.exp(cum_end)

    even = lax.broadcasted_iota(jnp.int32, (length, LANES), 1) < dh
    rows2 = lax.broadcasted_iota(jnp.int32, (l2, l2), 0)
    cols2 = lax.broadcasted_iota(jnp.int32, (l2, l2), 1)
    same = (rows2 >= length) == (cols2 >= length)
    strict_bd = same & (rows2 > cols2)
    incl_bd = same & (rows2 >= cols2)
    rows_s = lax.broadcasted_iota(jnp.int32, (LANES, LANES), 0)
    cols_s = lax.broadcasted_iota(jnp.int32, (LANES, LANES), 1)
    state_bd = (rows_s >= dh) == (cols_s >= dh)
    n_double = int(math.log2(length))
    assert 2 ** n_double == length

    def stack(x):
        return jnp.concatenate([jnp.where(even, x, 0.0), jnp.where(even, 0.0, x)], axis=0)

    n_mats, xs_u, uy0s, v_stacks, abks, kas = [], [], [], [], [], []
    for p in range(n_pair):
        sl = slice(p * LANES, (p + 1) * LANES)
        kk_r = kk_raw[:, sl]
        kk = kk_r / jnp.maximum(jnp.sqrt(_pair_sum(kk_r * kk_r, even)), 1e-12)
        ka = kk * a[:, sl]
        lhs = _bf(jnp.concatenate([stack(-kk * p_ex[:, sl]), stack(r[:, sl] * p_in[:, sl])], axis=0))
        bh = _bf(ka * p_inv[:, sl])
        kh = _bf(k2[:, sl] * p_inv[:, sl])
        rhs = jnp.concatenate([bh, bh, kh, kh], axis=0)
        gram = _dot_nt(lhs, rhs)
        n_mats.append(jnp.where(strict_bd, gram[0:l2, 0:l2], 0.0))
        m_bd = jnp.where(strict_bd, gram[0:l2, l2:], 0.0)
        abks.append(_bf(jnp.concatenate([jnp.where(incl_bd, gram[l2:, 0:l2], 0.0),
                                         jnp.where(incl_bd, gram[l2:, l2:], 0.0)], axis=1)))
        uy0 = _dot_nt(lhs, _bf(bds[p]))
        v_stack = _bf(stack(v[:, sl]))
        xs_u.append(uy0[0:l2] + _dot(_bf(m_bd), v_stack))
        uy0s.append(uy0[l2:])
        v_stacks.append(v_stack)
        kas.append(ka)

    for step in range(n_double):
        for p in range(n_pair):
            n_b = _bf(n_mats[p])
            xs_u[p] = xs_u[p] + _dot(n_b, _bf(xs_u[p]))
            if step + 1 < n_double:
                n_mats[p] = _dot(n_b, n_b)

    new_bd = []
    for p in range(n_pair):
        sl = slice(p * LANES, (p + 1) * LANES)
        u_st = xs_u[p]
        y_st = uy0s[p] + _dot(abks[p], jnp.concatenate([_bf(u_st), v_stacks[p]], axis=0))
        y = y_st[0:length] + y_st[length:]
        u_pair = u_st[0:length] + u_st[length:]
        v_p = v[:, sl]
        upd = _dot_tn(_bf(jnp.concatenate([u_pair, v_p], axis=0)),
                      _bf(jnp.concatenate([kas[p] * p_end[:, sl], k2[:, sl] * p_end[:, sl]], axis=0)))
        new_bd.append(bds[p] * dec_end[:, sl] + jnp.where(state_bd, upd, 0.0))

        mu = _pair_sum(y, even) * (1.0 / dh)
        cen = y - mu
        var = _pair_sum(cen * cen, even) * (1.0 / dh)
        y = cen * lax.rsqrt(var + RWKV_GN_EPS) * gg_ref[:, sl] + gb_ref[:, sl]
        bonus = _pair_sum(rk_bonus[:, sl], even) * v_p
        y_ref[:, sl] = _bf((y + bonus) * _silu(z[:, sl]))

    for p in range(n_pair):
        bd_scr[p] = new_bd[p]

    @pl.when(c_idx == pl.num_programs(1) - 1)
    def _():
        for p in range(n_pair):
            s_out[0, 2 * p] = bd_scr[p, 0:dh, 0:dh]
            s_out[0, 2 * p + 1] = bd_scr[p, dh:2 * dh, dh:2 * dh]
        shift_out[0] = carry_scr[...]


def _rwkv_call(pc, p, s0, shift0, *, n_seq, n_chunks, length, row_off, name):
    assert row_off % length == 0
    blk0 = row_off // length
    rows = n_seq * n_chunks * length
    vec = pl.BlockSpec((1, RWKV_W), lambda b, c: (0, 0))
    st4 = lambda b, c: (b, 0, 0, 0)
    st3 = lambda b, c: (b, 0, 0)
    return pl.pallas_call(
        functools.partial(_rwkv_kernel, length=length),
        grid=(n_seq, n_chunks),
        in_specs=[
            pl.BlockSpec((length, C_COLS), lambda b, c: (blk0 + b * n_chunks + c, 0)),
            pl.BlockSpec((1, C_COLS), lambda b, c: (0, 0)),
            pl.BlockSpec((2 * RWKV_RANK, 2 * RWKV_W), lambda b, c: (0, 0)),
            vec, vec, vec, vec, vec, vec, vec,
            pl.BlockSpec((1, RWKV_H, RWKV_DH, RWKV_DH), st4),
            pl.BlockSpec((1, 1, C_COLS), st3),
        ],
        out_specs=[
            pl.BlockSpec((length, RWKV_W), lambda b, c: (b * n_chunks + c, 0)),
            pl.BlockSpec((1, RWKV_H, RWKV_DH, RWKV_DH), st4),
            pl.BlockSpec((1, 1, C_COLS), st3),
        ],
        out_shape=[
            jax.ShapeDtypeStruct((rows, RWKV_W), BF16),
            jax.ShapeDtypeStruct((n_seq, RWKV_H, RWKV_DH, RWKV_DH), F32),
            jax.ShapeDtypeStruct((n_seq, 1, C_COLS), F32),
        ],
        scratch_shapes=[pltpu.VMEM((RWKV_H // 2, 2 * RWKV_DH, 2 * RWKV_DH), F32),
                        pltpu.VMEM((1, C_COLS), F32)],
        compiler_params=pltpu.CompilerParams(
            dimension_semantics=("parallel", "arbitrary"), vmem_limit_bytes=VMEM_LIMIT),
        name=name,
    )(pc, p['mu'], p['wa_up'], p['w0'], p['a0'], p['k_k'], p['k_a'], p['r_k'], p['gn_g'], p['gn_b'],
      s0, shift0)


def _prep_layer(l, lb_all, norm_pre, norm_post, w_in, mlstm_b_i, mlstm_b_f, mlstm_norm, hgrn_norm,
                rwkv_mu, rwkv_w0, rwkv_w_up, rwkv_a0, rwkv_a_up, rwkv_k_k, rwkv_k_a, rwkv_r_k,
                rwkv_gn_g, rwkv_gn_b, w_proj_a, w_proj_b, w_proj_c, w_out):
    w = w_in[l]
    a_cols = 5 * MLSTM_W + 2 * MLSTM_H
    if0 = 3 * MLSTM_W
    b0 = a_cols
    c0 = b0 + B_W
    g0 = c0 + C_COLS
    w_a = _bf(jnp.concatenate([w[:, 0:if0], w[:, if0 + 2 * MLSTM_H:a_cols]], axis=1))
    w_b = _bf(w[:, b0:c0])
    w_c = _bf(jnp.concatenate(
        [w[:, c0:g0], w[:, if0:if0 + 2 * MLSTM_H],
         jnp.zeros((D_MODEL, LANES - 2 * MLSTM_H), F32)], axis=1))
    w_g = _bf(w[:, g0:])
    bias = jnp.concatenate([mlstm_b_i[l], mlstm_b_f[l],
                            jnp.zeros((LANES - 2 * MLSTM_H,), F32)]).reshape(1, LANES)
    zero = jnp.zeros((RWKV_RANK, RWKV_W), F32)
    wa_up = _bf(jnp.concatenate([jnp.concatenate([rwkv_w_up[l], zero], axis=1),
                                 jnp.concatenate([zero, rwkv_a_up[l]], axis=1)], axis=0))
    row = lambda a: a.reshape(1, -1).astype(F32)
    return dict(
        norm_pre=row(norm_pre[l]), norm_post=row(norm_post[l]), w_a=w_a, w_b=w_b, w_c=w_c, w_g=w_g,
        bias=bias, mlstm_norm=row(mlstm_norm[l]), lb=row(lb_all[l]), hgrn_norm=row(hgrn_norm[l]),
        rwkv=dict(mu=row(rwkv_mu[l]), wa_up=wa_up, w0=row(rwkv_w0[l]), a0=row(rwkv_a0[l]),
                  k_k=row(rwkv_k_k[l]), k_a=row(rwkv_k_a[l]), r_k=row(rwkv_r_k[l]),
                  gn_g=row(rwkv_gn_g[l]), gn_b=row(rwkv_gn_b[l])),
        wpa=_bf(w_proj_a[l]), wpb=_bf(w_proj_b[l]), wpc=_bf(w_proj_c[l]), wo=_bf(w_out[l]))


def _lane_bcast(m):
    return jnp.broadcast_to(m[..., None], m.shape + (LANES,))


def _branches(p, proj_main, proj_tail, st_s, *, bp, t_p, bs, t_s, tag):
    pa_m, pb_m, pc_m = proj_main
    pa_t, pb_t, pc_t = proj_tail
    c_s, n_s, m_s, sh_s, sr_s, shift_s = st_s
    n_chunks = t_p // CHUNK
    meta_rows = bp * N_META
    zeros = lambda *s: jnp.zeros(s, F32)

    ya_meta, c, n, m = _mlstm_call(
        pa_t, pc_t, p['bias'], p['mlstm_norm'], zeros(bp, MLSTM_H, MLSTM_DH, MLSTM_DH),
        zeros(bp, MLSTM_H, MLSTM_DH), zeros(bp, MLSTM_H, LANES),
        n_seq=bp, n_chunks=1, length=N_META, row_off=0, name=f'mlstm_meta{tag}')
    ya_main, c_p, n_p, m_p = _mlstm_call(
        pa_m, pc_m, p['bias'], p['mlstm_norm'], c, n, m,
        n_seq=bp, n_chunks=n_chunks, length=CHUNK, row_off=0, name=f'mlstm_main{tag}')
    ya_samp, c_so, n_so, m_so = _mlstm_call(
        pa_t, pc_t, p['bias'], p['mlstm_norm'], c_s, n_s, _lane_bcast(m_s),
        n_seq=bs, n_chunks=1, length=t_s, row_off=meta_rows, name=f'mlstm_samp{tag}')

    yb_meta, s = _hgrn_call(pb_t, p['lb'], p['hgrn_norm'], zeros(bp, HGRN_H, HGRN_DK, HGRN_DV),
                            n_seq=bp, n_chunks=1, length=N_META, row_off=0, name=f'hgrn_meta{tag}')
    yb_main, sh_p = _hgrn_call(pb_m, p['lb'], p['hgrn_norm'], s,
                               n_seq=bp, n_chunks=n_chunks, length=CHUNK, row_off=0,
                               name=f'hgrn_main{tag}')
    yb_samp, sh_so = _hgrn_call(pb_t, p['lb'], p['hgrn_norm'], sh_s,
                                n_seq=bs, n_chunks=1, length=t_s, row_off=meta_rows,
                                name=f'hgrn_samp{tag}')

    yc_meta, s, shift = _rwkv_call(pc_t, p['rwkv'], zeros(bp, RWKV_H, RWKV_DH, RWKV_DH),
                                   zeros(bp, 1, C_COLS), n_seq=bp, n_chunks=1, length=N_META,
                                   row_off=0, name=f'rwkv_meta{tag}')
    yc_main, sr_p, shift_p = _rwkv_call(pc_m, p['rwkv'], s, shift, n_seq=bp, n_chunks=n_chunks,
                                        length=CHUNK, row_off=0, name=f'rwkv_main{tag}')
    yc_samp, sr_so, shift_so = _rwkv_call(pc_t, p['rwkv'], sr_s, shift_s, n_seq=bs, n_chunks=1,
                                          length=t_s, row_off=meta_rows, name=f'rwkv_samp{tag}')

    y_main = (ya_main, yb_main, yc_main)
    y_tail = tuple(jnp.concatenate([a, b], axis=0)
                   for a, b in ((ya_meta, ya_samp), (yb_meta, yb_samp), (yc_meta, yc_samp)))
    st_p_out = (c_p, n_p, m_p[:, :, 0], sh_p, sr_p, shift_p)
    st_s_out = (c_so, n_so, m_so[:, :, 0], sh_so, sr_so, shift_so)
    return y_main, y_tail, st_p_out, st_s_out


def kernel(x_prompt, x_sample, state_mlstm_C, state_mlstm_n, state_mlstm_m, state_hgrn_S,
           state_rwkv_S, cache_rwkv_shift, meta_tokens, norm_pre, norm_post, w_in,
           mlstm_b_i, mlstm_b_f, mlstm_norm, hgrn_lb_logits, hgrn_norm, rwkv_mu, rwkv_w0,
           rwkv_w_up, rwkv_a0, rwkv_a_up, rwkv_k_k, rwkv_k_a, rwkv_r_k, rwkv_gn_g, rwkv_gn_b,
           w_proj_a, w_proj_b, w_proj_c, w_out):
    bp, t_p, _ = x_prompt.shape
    bs, t_s, _ = x_sample.shape
    depth = w_in.shape[0]
    assert t_p % CHUNK == 0 and t_s % HGRN_SUB == 0 and (bp * N_META) % t_s == 0

    sm = jax.nn.softmax(hgrn_lb_logits.astype(F32), axis=0)
    lb_all = jnp.cumsum(sm, axis=0) - sm[0]

    x_main = x_prompt.reshape(bp * t_p, D_MODEL)
    meta = jnp.broadcast_to(meta_tokens.astype(F32)[None], (bp, N_META, D_MODEL))
    x_tail = jnp.concatenate([meta.reshape(bp * N_META, D_MODEL),
                              x_sample.reshape(bs * t_s, D_MODEL)], axis=0)

    outs_p, outs_s = [], []
    for l in range(depth):
        p = _prep_layer(l, lb_all, norm_pre, norm_post, w_in, mlstm_b_i, mlstm_b_f, mlstm_norm,
                        hgrn_norm, rwkv_mu, rwkv_w0, rwkv_w_up, rwkv_a0, rwkv_a_up, rwkv_k_k,
                        rwkv_k_a, rwkv_r_k, rwkv_gn_g, rwkv_gn_b, w_proj_a, w_proj_b, w_proj_c,
                        w_out)
        proj_main = tuple(_proj(x_main, p['norm_pre'], p[k], f'proj_{k}_main{l}')
                          for k in ('w_a', 'w_b', 'w_c'))
        proj_tail = tuple(_proj(x_tail, p['norm_pre'], p[k], f'proj_{k}_tail{l}')
                          for k in ('w_a', 'w_b', 'w_c'))
        st_s = (state_mlstm_C[l], state_mlstm_n[l], state_mlstm_m[l], state_hgrn_S[l],
                state_rwkv_S[l], cache_rwkv_shift[l])
        y_main, y_tail, st_p_out, st_s_out = _branches(
            p, proj_main, proj_tail, st_s, bp=bp, t_p=t_p, bs=bs, t_s=t_s, tag=str(l))
        x_main = _merge(x_main, p['norm_pre'], *y_main, p['w_g'], p['wpa'], p['wpb'], p['wpc'],
                        p['wo'], p['norm_post'], f'merge_main{l}')
        x_tail = _merge(x_tail, p['norm_pre'], *y_tail, p['w_g'], p['wpa'], p['wpb'], p['wpc'],
                        p['wo'], p['norm_post'], f'merge_tail{l}')
        outs_p.append(st_p_out)
        outs_s.append(st_s_out)

    states_p = tuple(jnp.stack([o[j] for o in outs_p]) for j in range(6))
    states_s = tuple(jnp.stack([o[j] for o in outs_s]) for j in range(6))
    y_prompt = x_main.reshape(bp, t_p, D_MODEL)
    y_sample = x_tail[bp * N_META:].reshape(bs, t_s, D_MODEL)
    return (y_prompt, y_sample) + states_p + states_s
```

```python
import functools
import math

import jax
import jax.numpy as jnp
from jax import lax
from jax.experimental import pallas as pl
from jax.experimental.pallas import tpu as pltpu

F32 = jnp.float32
BF16 = jnp.bfloat16

D_MODEL = 2048
CHUNK = 64
N_META = 16
EPS = 1e-6

MLSTM_H = 8
MLSTM_DH = 128
MLSTM_W = MLSTM_H * MLSTM_DH
HGRN_H = 4
HGRN_DK = 128
HGRN_DV = 128
HGRN_W = HGRN_H * HGRN_DV
RWKV_H = 8
RWKV_DH = 64
RWKV_W = RWKV_H * RWKV_DH
RWKV_RANK = 64
RWKV_GN_EPS = 64e-5
C_COLS = 4 * RWKV_W + 2 * RWKV_RANK

LANES = 128
HGRN_SUB = 16
A_W = 5 * MLSTM_W
B_W = 4 * HGRN_W
C_W = C_COLS + 3 * LANES
IF_BLOCK = C_COLS // LANES
MERGE_NC = 512
RWKV_GROUP = 2
VMEM_LIMIT = 56 * 1024 * 1024


def _dot(a, b):
    return jnp.dot(a, b, preferred_element_type=F32)


def _dot_nt(a, b):
    return lax.dot_general(a, b, (((1,), (1,)), ((), ())), preferred_element_type=F32)


def _dot_tn(a, b):
    return lax.dot_general(a, b, (((0,), (0,)), ((), ())), preferred_element_type=F32)


def _bf(a):
    return a.astype(BF16)


def _sigmoid(x):
    return 1.0 / (1.0 + jnp.exp(-x))


def _silu(x):
    return x * _sigmoid(x)


def _softplus(x):
    return jnp.maximum(x, 0.0) + jnp.log1p(jnp.exp(-jnp.abs(x)))


def _tri(length, strict=False):
    row = lax.broadcasted_iota(jnp.int32, (length, length), 0)
    col = lax.broadcasted_iota(jnp.int32, (length, length), 1)
    return (row > col) if strict else (row >= col)


def _cumsum_time(x, tri_b):
    hi = _bf(x)
    r1 = x - hi.astype(F32)
    mid = _bf(r1)
    lo = _bf(r1 - mid.astype(F32))
    return _dot(tri_b, hi) + _dot(tri_b, mid) + _dot(tri_b, lo)


def _rmsnorm(x, g):
    return x * lax.rsqrt(jnp.mean(x * x, axis=-1, keepdims=True) + EPS) * g


def _row_tile(rows, cap):
    best = None
    for t in range(16, min(rows, cap) + 1, 16):
        if rows % t == 0:
            best = t
    assert best is not None, rows
    return best


def _col_tile(cols, cap):
    best = None
    for t in range(LANES, min(cols, cap) + 1, LANES):
        if cols % t == 0:
            best = t
    assert best is not None, cols
    return best


def _proj_kernel(x_ref, g_ref, w_ref, o_ref):
    h = _bf(_rmsnorm(x_ref[...], g_ref[...]))
    o_ref[...] = _dot(h, w_ref[...])


def _proj(x, g, w, name):
    rows, _ = x.shape
    cols = w.shape[1]
    tm = _row_tile(rows, 1024)
    tn = _col_tile(cols, 1280)
    return pl.pallas_call(
        _proj_kernel,
        grid=(cols // tn, rows // tm),
        in_specs=[
            pl.BlockSpec((tm, D_MODEL), lambda j, i: (i, 0)),
            pl.BlockSpec((1, D_MODEL), lambda j, i: (0, 0)),
            pl.BlockSpec((D_MODEL, tn), lambda j, i: (0, j)),
        ],
        out_specs=pl.BlockSpec((tm, tn), lambda j, i: (i, j)),
        out_shape=jax.ShapeDtypeStruct((rows, cols), F32),
        compiler_params=pltpu.CompilerParams(
            dimension_semantics=("parallel", "parallel"), vmem_limit_bytes=VMEM_LIMIT),
        name=name,
    )(x, g, w)


def _merge_kernel(x_ref, gpre_ref, ya_ref, yb_ref, yc_ref, wg_ref,
                  wpa_ref, wpb_ref, wpc_ref, wo_ref, gpost_ref, o_ref, h_scr, acc_scr):
    j = pl.program_id(1)
    nc = MERGE_NC

    @pl.when(j == 0)
    def _():
        h_scr[...] = _bf(_rmsnorm(x_ref[...], gpre_ref[...]))
        acc_scr[...] = jnp.zeros_like(acc_scr)

    gates = _sigmoid(_dot(h_scr[...], wg_ref[...]))
    merged = (gates[:, 0:nc] * _dot(ya_ref[...], wpa_ref[...])
              + gates[:, nc:2 * nc] * _dot(yb_ref[...], wpb_ref[...])
              + gates[:, 2 * nc:3 * nc] * _dot(yc_ref[...], wpc_ref[...]))
    acc_scr[...] += _dot(_bf(merged), wo_ref[...])

    @pl.when(j == pl.num_programs(1) - 1)
    def _():
        o_ref[...] = x_ref[...] + _rmsnorm(acc_scr[...], gpost_ref[...])


def _merge(x, gpre, ya, yb, yc, wg, wpa, wpb, wpc, wo, gpost, name):
    rows = x.shape[0]
    tm = _row_tile(rows, 640)
    nc = MERGE_NC
    n_chunk = D_MODEL // nc
    row = lambda i, j: (i, 0)
    return pl.pallas_call(
        _merge_kernel,
        grid=(rows // tm, n_chunk),
        in_specs=[
            pl.BlockSpec((tm, D_MODEL), row),
            pl.BlockSpec((1, D_MODEL), lambda i, j: (0, 0)),
            pl.BlockSpec((tm, MLSTM_W), row),
            pl.BlockSpec((tm, HGRN_W), row),
            pl.BlockSpec((tm, RWKV_W), row),
            pl.BlockSpec((D_MODEL, 3 * nc), lambda i, j: (0, j)),
            pl.BlockSpec((MLSTM_W, nc), lambda i, j: (0, j)),
            pl.BlockSpec((HGRN_W, nc), lambda i, j: (0, j)),
            pl.BlockSpec((RWKV_W, nc), lambda i, j: (0, j)),
            pl.BlockSpec((nc, D_MODEL), lambda i, j: (j, 0)),
            pl.BlockSpec((1, D_MODEL), lambda i, j: (0, 0)),
        ],
        out_specs=pl.BlockSpec((tm, D_MODEL), row),
        out_shape=jax.ShapeDtypeStruct((rows, D_MODEL), F32),
        scratch_shapes=[pltpu.VMEM((tm, D_MODEL), BF16), pltpu.VMEM((tm, D_MODEL), F32)],
        compiler_params=pltpu.CompilerParams(
            dimension_semantics=("parallel", "arbitrary"), vmem_limit_bytes=VMEM_LIMIT),
        name=name,
    )(x, gpre, ya, yb, yc, wg, wpa, wpb, wpc, wo, gpost)


def _lane_mean(x, j_b):
    n = x.shape[0]
    hi = _bf(x)
    mid = _bf(x - hi.astype(F32))
    out = _dot(jnp.concatenate([hi, mid], axis=0), j_b)
    return out[0:n] + out[n:]


def _mlstm_kernel(q_ref, k_ref, v_ref, o_ref, z_ref, i_ref, f_ref, bias_i_ref, bias_f_ref, norm_ref,
                  c0_ref, n0_ref, m0_ref, y_ref, c_out, n_out, m_out, cn_scr, m_scr, *, length):
    c_idx = pl.program_id(1)
    dh = MLSTM_DH

    @pl.when(c_idx == 0)
    def _():
        n_t = n0_ref[0].T
        for h in range(MLSTM_H):
            cn_scr[h, :, 0:dh] = c0_ref[0, h].T
            cn_scr[h, :, dh:2 * dh] = jnp.broadcast_to(n_t[:, h:h + 1], (dh, dh))
        m_scr[...] = m0_ref[0]

    cns = [cn_scr[h] for h in range(MLSTM_H)]
    m_prev = m_scr[...]

    causal = _tri(length)
    tri_b = _bf(jnp.where(causal, 1.0, 0.0))
    ig = i_ref[...] + bias_i_ref[...]
    fg = f_ref[...] + bias_f_ref[...]
    log_f = jnp.minimum(fg, 0.0) - jnp.log1p(jnp.exp(-jnp.abs(fg)))
    b_all = _cumsum_time(log_f, tri_b)
    c_all = ig - b_all
    row = lax.broadcasted_iota(jnp.int32, (length, LANES), 0)
    run_max = c_all
    shift = 1
    while shift < length:
        run_max = jnp.maximum(
            run_max, jnp.where(row >= shift, pltpu.roll(run_max, shift, 0), -jnp.inf))
        shift *= 2
    mx_all = jnp.maximum(run_max, m_prev)
    mt_all = b_all + mx_all
    mx_end = mx_all[length - 1:length]
    m_new = b_all[length - 1:length] + mx_end
    w_old_all = jnp.exp(m_prev - mx_end)
    c_t = c_all.T
    ones = jnp.ones((length, dh), F32)
    j_b = jnp.full((dh, dh), 1.0 / dh, BF16)
    scale = MLSTM_DH ** -0.5

    heads = range(MLSTM_H)
    sls = [slice(h * dh, (h + 1) * dh) for h in heads]
    k_ts = [(k_ref[:, sl] * scale).T for sl in sls]
    qbs = [_bf(q_ref[:, sl]) for sl in sls]
    s_raw = [_dot(qbs[h], _bf(k_ts[h])) for h in heads]
    inter = [_dot(qbs[h], _bf(cns[h])) for h in heads]
    mx_bs = [jnp.broadcast_to(mx_all[:, h:h + 1], (length, dh)) for h in heads]
    es = [jnp.exp(jnp.where(causal, c_t[h:h + 1, :] - mx_bs[h][:, 0:length], -jnp.inf))
          for h in heads]
    v1s = [_bf(jnp.concatenate([v_ref[:, sl], ones], axis=1)) for sl in sls]
    tots = []
    for h in heads:
        w_inter = jnp.exp(m_prev[:, h:h + 1] - mx_bs[h])
        tots.append(jnp.concatenate([w_inter, w_inter], axis=1) * inter[h]
                    + _dot(_bf(s_raw[h] * es[h]), v1s[h]))
    new_cn = [w_old_all[:, h:h + 1] * cns[h]
              + _dot(_bf(k_ts[h] * es[h][length - 1:length, :]), v1s[h]) for h in heads]
    cens = [tots[h][:, 0:dh] - _lane_mean(tots[h][:, 0:dh], j_b) for h in heads]
    vars_ = [_lane_mean(cens[h] * cens[h], j_b) for h in heads]
    outs = []
    for h in heads:
        mt_b = jnp.broadcast_to(mt_all[:, h:h + 1], (length, dh))
        inv = 1.0 / jnp.maximum(jnp.abs(tots[h][:, dh:]), jnp.exp(-mt_b))
        hid = cens[h] * inv * lax.rsqrt(vars_[h] * inv * inv + EPS) * norm_ref[:, sls[h]]
        outs.append(_bf(hid * _sigmoid(o_ref[:, sls[h]]) * _silu(z_ref[:, sls[h]])))

    y_ref[...] = jnp.concatenate(outs, axis=1)
    for h in range(MLSTM_H):
        cn_scr[h] = new_cn[h]
    m_scr[...] = m_new

    @pl.when(c_idx == pl.num_programs(1) - 1)
    def _():
        for h in range(MLSTM_H):
            c_out[0, h] = cn_scr[h, :, 0:dh].T
            n_out[0, h:h + 1, :] = cn_scr[h, :, dh:2 * dh].T[0:1, :]
        m_out[0] = m_scr[...]


def _mlstm_call(pa, pc, bias_i, bias_f, norm, c0, n0, m0, *, n_seq, n_chunks, length, row_off, name):
    assert row_off % length == 0
    blk0 = row_off // length
    rows = n_seq * n_chunks * length

    def col(j):
        return lambda b, c: (blk0 + b * n_chunks + c, j)

    st4 = lambda b, c: (b, 0, 0, 0)
    st3 = lambda b, c: (b, 0, 0)
    lane_row = pl.BlockSpec((1, LANES), lambda b, c: (0, 0))
    return pl.pallas_call(
        functools.partial(_mlstm_kernel, length=length),
        grid=(n_seq, n_chunks),
        in_specs=[
            pl.BlockSpec((length, MLSTM_W), col(0)),
            pl.BlockSpec((length, MLSTM_W), col(1)),
            pl.BlockSpec((length, MLSTM_W), col(2)),
            pl.BlockSpec((length, MLSTM_W), col(3)),
            pl.BlockSpec((length, MLSTM_W), col(4)),
            pl.BlockSpec((length, LANES), col(IF_BLOCK)),
            pl.BlockSpec((length, LANES), col(IF_BLOCK + 1)),
            lane_row, lane_row,
            pl.BlockSpec((1, MLSTM_W), lambda b, c: (0, 0)),
            pl.BlockSpec((1, MLSTM_H, MLSTM_DH, MLSTM_DH), st4),
            pl.BlockSpec((1, MLSTM_H, MLSTM_DH), st3),
            pl.BlockSpec((1, 1, LANES), st3),
        ],
        out_specs=[
            pl.BlockSpec((length, MLSTM_W), lambda b, c: (b * n_chunks + c, 0)),
            pl.BlockSpec((1, MLSTM_H, MLSTM_DH, MLSTM_DH), st4),
            pl.BlockSpec((1, MLSTM_H, MLSTM_DH), st3),
            pl.BlockSpec((1, 1, LANES), st3),
        ],
        out_shape=[
            jax.ShapeDtypeStruct((rows, MLSTM_W), BF16),
            jax.ShapeDtypeStruct((n_seq, MLSTM_H, MLSTM_DH, MLSTM_DH), F32),
            jax.ShapeDtypeStruct((n_seq, MLSTM_H, MLSTM_DH), F32),
            jax.ShapeDtypeStruct((n_seq, 1, LANES), F32),
        ],
        scratch_shapes=[pltpu.VMEM((MLSTM_H, MLSTM_DH, 2 * MLSTM_DH), F32),
                        pltpu.VMEM((1, LANES), F32)],
        compiler_params=pltpu.CompilerParams(
            dimension_semantics=("parallel", "arbitrary"), vmem_limit_bytes=VMEM_LIMIT),
        name=name,
    )(pa, pa, pa, pa, pa, pc, pc, bias_i, bias_f, norm, c0, n0, m0)


def _hgrn_kernel(q_ref, f_ref, i_ref, z_ref, lb_ref, norm_ref, s0_ref, y_ref, s_out,
                 st_scr, *, length):
    c_idx = pl.program_id(1)
    dk = HGRN_DK
    sub = min(HGRN_SUB, length)
    n_sub = length // sub

    @pl.when(c_idx == 0)
    def _():
        for h in range(HGRN_H):
            st_scr[h] = s0_ref[0, h].T

    sts = [st_scr[h] for h in range(HGRN_H)]

    lb = lb_ref[...]
    fp = f_ref[...]
    f_gate = lb + (1.0 - lb) * _sigmoid(fp)
    k_all = (1.0 - lb) * _sigmoid(-fp)
    tri_b = _bf(jnp.where(_tri(length), 1.0, 0.0))
    a_all = _cumsum_time(jnp.log(f_gate), tri_b)
    lane_l = lax.broadcasted_iota(jnp.int32, (sub, length), 1)
    row_l = lax.broadcasted_iota(jnp.int32, (sub, length), 0)

    heads = range(HGRN_H)
    sls = [slice(h * dk, (h + 1) * dk) for h in heads]
    qs = [q_ref[:, sl] for sl in sls]
    ks = [k_all[:, sl] for sl in sls]
    avs = [a_all[:, sl] for sl in sls]
    vbs = [_bf(i_ref[:, sl]) for sl in sls]
    o_inter = [_dot_nt(_bf(qs[h] * jnp.exp(avs[h])), _bf(sts[h])) for h in heads]
    new_st = []
    for h in heads:
        a_end = avs[h][length - 1:length]
        new_st.append(jnp.exp(a_end) * sts[h]
                      + _dot_tn(vbs[h], _bf(ks[h] * jnp.exp(a_end - avs[h]))))

    blocks = [[] for _ in heads]
    for i in range(n_sub):
        r0 = i * sub
        blks = []
        for h in heads:
            if i > 0:
                ref_row = avs[h][r0:r0 + 1]
                q_s = _bf(qs[h][r0:r0 + sub] * jnp.exp(avs[h][r0:r0 + sub] - ref_row))
                k_s = jnp.concatenate(
                    [_bf(ks[h][0:r0] * jnp.exp(ref_row - avs[h][0:r0])),
                     jnp.zeros((length - r0, dk), BF16)], axis=0)
                blks.append(_dot_nt(q_s, k_s))
            else:
                blks.append(jnp.zeros((sub, length), F32))
        for s_idx in range(sub):
            for h in heads:
                a_i = avs[h][r0:r0 + sub]
                e = jnp.exp(jnp.minimum(a_i - a_i[s_idx:s_idx + 1], 0.0))
                col_v = jnp.sum(qs[h][r0:r0 + sub] * e * ks[h][r0 + s_idx:r0 + s_idx + 1],
                                axis=-1, keepdims=True)
                blks[h] = jnp.where(lane_l == r0 + s_idx, col_v, blks[h])
        for h in heads:
            blocks[h].append(jnp.where(lane_l <= r0 + row_l, blks[h], 0.0))

    outs = []
    for h in heads:
        scores = jnp.concatenate(blocks[h], axis=0) if n_sub > 1 else blocks[h][0]
        o = o_inter[h] + _dot(_bf(scores), vbs[h])
        o = o * lax.rsqrt(jnp.mean(o * o, axis=-1, keepdims=True) + EPS)
        outs.append(_bf(o * norm_ref[:, sls[h]] * _silu(z_ref[:, sls[h]])))

    y_ref[...] = jnp.concatenate(outs, axis=1)
    for h in range(HGRN_H):
        st_scr[h] = new_st[h]

    @pl.when(c_idx == pl.num_programs(1) - 1)
    def _():
        for h in range(HGRN_H):
            s_out[0, h] = st_scr[h].T


def _hgrn_call(pb, lb, norm, s0, *, n_seq, n_chunks, length, row_off, name):
    assert row_off % length == 0
    blk0 = row_off // length
    rows = n_seq * n_chunks * length

    def col(j):
        return lambda b, c: (blk0 + b * n_chunks + c, j)

    st4 = lambda b, c: (b, 0, 0, 0)
    return pl.pallas_call(
        functools.partial(_hgrn_kernel, length=length),
        grid=(n_seq, n_chunks),
        in_specs=[
            pl.BlockSpec((length, HGRN_W), col(0)),
            pl.BlockSpec((length, HGRN_W), col(1)),
            pl.BlockSpec((length, HGRN_W), col(2)),
            pl.BlockSpec((length, HGRN_W), col(3)),
            pl.BlockSpec((1, HGRN_W), lambda b, c: (0, 0)),
            pl.BlockSpec((1, HGRN_W), lambda b, c: (0, 0)),
            pl.BlockSpec((1, HGRN_H, HGRN_DK, HGRN_DV), st4),
        ],
        out_specs=[
            pl.BlockSpec((length, HGRN_W), lambda b, c: (b * n_chunks + c, 0)),
            pl.BlockSpec((1, HGRN_H, HGRN_DK, HGRN_DV), st4),
        ],
        out_shape=[
            jax.ShapeDtypeStruct((rows, HGRN_W), BF16),
            jax.ShapeDtypeStruct((n_seq, HGRN_H, HGRN_DK, HGRN_DV), F32),
        ],
        scratch_shapes=[pltpu.VMEM((HGRN_H, HGRN_DV, HGRN_DK), F32)],
        compiler_params=pltpu.CompilerParams(
            dimension_semantics=("parallel", "arbitrary"), vmem_limit_bytes=VMEM_LIMIT),
        name=name,
    )(pb, pb, pb, pb, lb, norm, s0)


def _pair_sum(x, even):
    s_even = jnp.sum(jnp.where(even, x, 0.0), axis=-1, keepdims=True)
    s_odd = jnp.sum(jnp.where(even, 0.0, x), axis=-1, keepdims=True)
    return jnp.where(even, s_even, s_odd)


def _rwkv_kernel(pc_ref, mu_ref, wa_up_ref, w0_ref, a0_ref, kk_ref, ka_ref, rk_ref, gg_ref, gb_ref,
                 s0_ref, shift0_ref, y_ref, s_out, shift_out, bd_scr, carry_scr, *, length, n_grp):
    c_idx = pl.program_id(1)
    dh = RWKV_DH
    w = RWKV_W
    n_pair = RWKV_H // 2
    l2 = 2 * length
    n_rows = n_grp * length

    @pl.when(c_idx == 0)
    def _():
        zero = jnp.zeros((dh, dh), F32)
        for g in range(n_grp):
            for p in range(n_pair):
                top = jnp.concatenate([s0_ref[g, 2 * p], zero], axis=1)
                bot = jnp.concatenate([zero, s0_ref[g, 2 * p + 1]], axis=1)
                bd_scr[g * n_pair + p] = jnp.concatenate([top, bot], axis=0)
            carry_scr[g:g + 1, :] = shift0_ref[g]

    bds = [bd_scr[i] for i in range(n_grp * n_pair)]

    pc = pc_ref[...].reshape(n_rows, C_COLS)
    row_id = lax.broadcasted_iota(jnp.int32, (n_rows, C_COLS), 0)
    prev = pltpu.roll(pc, 1, 0)
    for g in range(n_grp):
        prev = jnp.where(row_id == g * length, carry_scr[g:g + 1, :], prev)
    xs = pc + mu_ref[...] * (prev - pc)
    r = xs[:, 0:w]
    k = xs[:, w:2 * w]
    v = xs[:, 2 * w:3 * w]
    low = xs[:, 3 * w:3 * w + 2 * RWKV_RANK]
    z = xs[:, 3 * w + 2 * RWKV_RANK:]
    lane = lax.broadcasted_iota(jnp.int32, (n_rows, 2 * RWKV_RANK), 1)
    low = jnp.where(lane < RWKV_RANK, jnp.tanh(low), low)
    up = _dot(_bf(low), wa_up_ref[...])
    w_logit = -_softplus(-(w0_ref[...] + up[:, 0:w])) - 0.5
    log_w = -jnp.exp(w_logit)
    a = _sigmoid(a0_ref[...] + up[:, w:2 * w])
    kk_raw = k * kk_ref[...]
    k2 = k * (1.0 + (a - 1.0) * ka_ref[...])
    rk_bonus = r * k2 * rk_ref[...]
    rows_g = lax.broadcasted_iota(jnp.int32, (n_rows, n_rows), 0)
    cols_g = lax.broadcasted_iota(jnp.int32, (n_rows, n_rows), 1)
    shift_l = int(math.log2(length))
    tri_g = (rows_g >= cols_g) & ((rows_g >> shift_l) == (cols_g >> shift_l))
    cum = _cumsum_time(log_w, _bf(jnp.where(tri_g, 1.0, 0.0)))
    p_in = jnp.exp(cum)
    p_ex = jnp.exp(cum - log_w)
    p_inv = jnp.exp(-cum)
    cum_ends = [cum[(g + 1) * length - 1:(g + 1) * length] for g in range(n_grp)]
    p_end = jnp.concatenate(
        [jnp.exp(cum_ends[g] - cum[g * length:(g + 1) * length]) for g in range(n_grp)], axis=0)
    dec_ends = [jnp.exp(ce) for ce in cum_ends]

    even = lax.broadcasted_iota(jnp.int32, (length, LANES), 1) < dh
    rows2 = lax.broadcasted_iota(jnp.int32, (l2, l2), 0)
    cols2 = lax.broadcasted_iota(jnp.int32, (l2, l2), 1)
    same = (rows2 >= length) == (cols2 >= length)
    strict_bd = same & (rows2 > cols2)
    incl_bd = same & (rows2 >= cols2)
    rows_s = lax.broadcasted_iota(jnp.int32, (LANES, LANES), 0)
    cols_s = lax.broadcasted_iota(jnp.int32, (LANES, LANES), 1)
    state_bd = (rows_s >= dh) == (cols_s >= dh)
    n_double = int(math.log2(length))
    assert 2 ** n_double == length

    def stack(x):
        return jnp.concatenate([jnp.where(even, x, 0.0), jnp.where(even, 0.0, x)], axis=0)

    chains = [(g, p) for g in range(n_grp) for p in range(n_pair)]
    n_mats, xs_u, uy0s, v_stacks, abks, kas = [], [], [], [], [], []
    for g, p in chains:
        rs = slice(g * length, (g + 1) * length)
        sl = slice(p * LANES, (p + 1) * LANES)
        kk_r = kk_raw[rs, sl]
        kk = kk_r / jnp.maximum(jnp.sqrt(_pair_sum(kk_r * kk_r, even)), 1e-12)
        ka = kk * a[rs, sl]
        lhs = _bf(jnp.concatenate([stack(-kk * p_ex[rs, sl]), stack(r[rs, sl] * p_in[rs, sl])],
                                  axis=0))
        bh = _bf(ka * p_inv[rs, sl])
        kh = _bf(k2[rs, sl] * p_inv[rs, sl])
        rhs = jnp.concatenate([bh, bh, kh, kh], axis=0)
        gram = _dot_nt(lhs, rhs)
        n_mats.append(jnp.where(strict_bd, gram[0:l2, 0:l2], 0.0))
        m_bd = jnp.where(strict_bd, gram[0:l2, l2:], 0.0)
        abks.append(_bf(jnp.concatenate([jnp.where(incl_bd, gram[l2:, 0:l2], 0.0),
                                         jnp.where(incl_bd, gram[l2:, l2:], 0.0)], axis=1)))
        uy0 = _dot_nt(lhs, _bf(bds[g * n_pair + p]))
        v_stack = _bf(stack(v[rs, sl]))
        xs_u.append(uy0[0:l2] + _dot(_bf(m_bd), v_stack))
        uy0s.append(uy0[l2:])
        v_stacks.append(v_stack)
        kas.append(ka)

    for step in range(n_double):
        for i in range(len(chains)):
            n_b = _bf(n_mats[i])
            xs_u[i] = xs_u[i] + _dot(n_b, _bf(xs_u[i]))
            if step + 1 < n_double:
                n_mats[i] = _dot(n_b, n_b)

    gn_g = gg_ref[...]
    gn_b = gb_ref[...]
    new_bd, outs = [], []
    for i, (g, p) in enumerate(chains):
        rs = slice(g * length, (g + 1) * length)
        sl = slice(p * LANES, (p + 1) * LANES)
        u_st = xs_u[i]
        y_st = uy0s[i] + _dot(abks[i], jnp.concatenate([_bf(u_st), v_stacks[i]], axis=0))
        y = y_st[0:length] + y_st[length:]
        u_pair = u_st[0:length] + u_st[length:]
        v_p = v[rs, sl]
        upd = _dot_tn(_bf(jnp.concatenate([u_pair, v_p], axis=0)),
                      _bf(jnp.concatenate([kas[i] * p_end[rs, sl], k2[rs, sl] * p_end[rs, sl]],
                                          axis=0)))
        new_bd.append(bds[i] * dec_ends[g][:, sl] + jnp.where(state_bd, upd, 0.0))

        mu = _pair_sum(y, even) * (1.0 / dh)
        cen = y - mu
        var = _pair_sum(cen * cen, even) * (1.0 / dh)
        y = cen * lax.rsqrt(var + RWKV_GN_EPS) * gn_g[:, sl] + gn_b[:, sl]
        bonus = _pair_sum(rk_bonus[rs, sl], even) * v_p
        outs.append(_bf((y + bonus) * _silu(z[rs, sl])))

    for g in range(n_grp):
        y_ref[g] = jnp.concatenate(outs[g * n_pair:(g + 1) * n_pair], axis=1)
        carry_scr[g:g + 1, :] = pc[(g + 1) * length - 1:(g + 1) * length, :]
    for i in range(len(chains)):
        bd_scr[i] = new_bd[i]

    @pl.when(c_idx == pl.num_programs(1) - 1)
    def _():
        for g in range(n_grp):
            for p in range(n_pair):
                s_out[g, 2 * p] = bd_scr[g * n_pair + p, 0:dh, 0:dh]
                s_out[g, 2 * p + 1] = bd_scr[g * n_pair + p, dh:2 * dh, dh:2 * dh]
            shift_out[g] = carry_scr[g:g + 1, :]


def _rwkv_call(pc, p, s0, shift0, *, n_seq, n_chunks, length, row_off, name):
    grp = RWKV_GROUP
    assert n_seq % grp == 0
    if n_chunks == 1:
        assert row_off % (grp * length) == 0
        blk0 = row_off // (grp * length)
        pc_in = pc
        pc_spec = pl.BlockSpec((grp * length, C_COLS), lambda b, c: (blk0 + b, 0))
    else:
        assert row_off == 0 and pc.shape[0] == n_seq * n_chunks * length
        pc_in = pc.reshape(n_seq, n_chunks * length, pc.shape[1])
        pc_spec = pl.BlockSpec((grp, length, C_COLS), lambda b, c: (b, c, 0))
    vec = pl.BlockSpec((1, RWKV_W), lambda b, c: (0, 0))
    st4 = lambda b, c: (b, 0, 0, 0)
    st3 = lambda b, c: (b, 0, 0)
    y, s_end, shift_end = pl.pallas_call(
        functools.partial(_rwkv_kernel, length=length, n_grp=grp),
        grid=(n_seq // grp, n_chunks),
        in_specs=[
            pc_spec,
            pl.BlockSpec((1, C_COLS), lambda b, c: (0, 0)),
            pl.BlockSpec((2 * RWKV_RANK, 2 * RWKV_W), lambda b, c: (0, 0)),
            vec, vec, vec, vec, vec, vec, vec,
            pl.BlockSpec((grp, RWKV_H, RWKV_DH, RWKV_DH), st4),
            pl.BlockSpec((grp, 1, C_COLS), st3),
        ],
        out_specs=[
            pl.BlockSpec((grp, length, RWKV_W), lambda b, c: (b, c, 0)),
            pl.BlockSpec((grp, RWKV_H, RWKV_DH, RWKV_DH), st4),
            pl.BlockSpec((grp, 1, C_COLS), st3),
        ],
        out_shape=[
            jax.ShapeDtypeStruct((n_seq, n_chunks * length, RWKV_W), BF16),
            jax.ShapeDtypeStruct((n_seq, RWKV_H, RWKV_DH, RWKV_DH), F32),
            jax.ShapeDtypeStruct((n_seq, 1, C_COLS), F32),
        ],
        scratch_shapes=[pltpu.VMEM((grp * RWKV_H // 2, 2 * RWKV_DH, 2 * RWKV_DH), F32),
                        pltpu.VMEM((grp, C_COLS), F32)],
        compiler_params=pltpu.CompilerParams(
            dimension_semantics=("parallel", "arbitrary"), vmem_limit_bytes=VMEM_LIMIT),
        name=name,
    )(pc_in, p['mu'], p['wa_up'], p['w0'], p['a0'], p['k_k'], p['k_a'], p['r_k'], p['gn_g'],
      p['gn_b'], s0, shift0)
    return y.reshape(n_seq * n_chunks * length, RWKV_W), s_end, shift_end


def _prep_layer(l, lb_all, norm_pre, norm_post, w_in, mlstm_b_i, mlstm_b_f, mlstm_norm, hgrn_norm,
                rwkv_mu, rwkv_w0, rwkv_w_up, rwkv_a0, rwkv_a_up, rwkv_k_k, rwkv_k_a, rwkv_r_k,
                rwkv_gn_g, rwkv_gn_b, w_proj_a, w_proj_b, w_proj_c, w_out):
    w = w_in[l]
    a_cols = 5 * MLSTM_W + 2 * MLSTM_H
    if0 = 3 * MLSTM_W
    b0 = a_cols
    c0 = b0 + B_W
    g0 = c0 + C_COLS
    w_a = _bf(jnp.concatenate([w[:, 0:if0], w[:, if0 + 2 * MLSTM_H:a_cols]], axis=1))
    w_b = _bf(w[:, b0:c0])
    gate_pad = jnp.zeros((D_MODEL, LANES - MLSTM_H), F32)
    w_c = _bf(jnp.concatenate(
        [w[:, c0:g0], w[:, if0:if0 + MLSTM_H], gate_pad,
         w[:, if0 + MLSTM_H:if0 + 2 * MLSTM_H], gate_pad,
         jnp.zeros((D_MODEL, LANES), F32)], axis=1))
    n_chunk = D_MODEL // MERGE_NC
    w_g = _bf(jnp.concatenate(
        [w[:, g0 + b * D_MODEL + j * MERGE_NC:g0 + b * D_MODEL + (j + 1) * MERGE_NC]
         for j in range(n_chunk) for b in range(3)], axis=1))
    lane_pad = jnp.zeros((LANES - MLSTM_H,), F32)
    bias_i = jnp.concatenate([mlstm_b_i[l].astype(F32), lane_pad]).reshape(1, LANES)
    bias_f = jnp.concatenate([mlstm_b_f[l].astype(F32), lane_pad]).reshape(1, LANES)
    zero = jnp.zeros((RWKV_RANK, RWKV_W), F32)
    wa_up = _bf(jnp.concatenate([jnp.concatenate([rwkv_w_up[l], zero], axis=1),
                                 jnp.concatenate([zero, rwkv_a_up[l]], axis=1)], axis=0))
    row = lambda a: a.reshape(1, -1).astype(F32)
    return dict(
        norm_pre=row(norm_pre[l]), norm_post=row(norm_post[l]), w_a=w_a, w_b=w_b, w_c=w_c, w_g=w_g,
        bias_i=bias_i, bias_f=bias_f, mlstm_norm=row(mlstm_norm[l]), lb=row(lb_all[l]),
        hgrn_norm=row(hgrn_norm[l]),
        rwkv=dict(mu=row(rwkv_mu[l]), wa_up=wa_up, w0=row(rwkv_w0[l]), a0=row(rwkv_a0[l]),
                  k_k=row(rwkv_k_k[l]), k_a=row(rwkv_k_a[l]), r_k=row(rwkv_r_k[l]),
                  gn_g=row(rwkv_gn_g[l]), gn_b=row(rwkv_gn_b[l])),
        wpa=_bf(w_proj_a[l]), wpb=_bf(w_proj_b[l]), wpc=_bf(w_proj_c[l]), wo=_bf(w_out[l]))


def _lane_pad(m):
    return jnp.pad(m.astype(F32), ((0, 0), (0, LANES - m.shape[1])))[:, None, :]


def _branches(p, proj_main, proj_tail, st_s, *, bp, t_p, bs, t_s, tag):
    pa_m, pb_m, pc_m = proj_main
    pa_t, pb_t, pc_t = proj_tail
    c_s, n_s, m_s, sh_s, sr_s, shift_s = st_s
    n_chunks = t_p // CHUNK
    meta_rows = bp * N_META
    zeros = lambda *s: jnp.zeros(s, F32)

    gate_args = (p['bias_i'], p['bias_f'], p['mlstm_norm'])
    ya_meta, c, n, m = _mlstm_call(
        pa_t, pc_t, *gate_args, zeros(bp, MLSTM_H, MLSTM_DH, MLSTM_DH),
        zeros(bp, MLSTM_H, MLSTM_DH), zeros(bp, 1, LANES),
        n_seq=bp, n_chunks=1, length=N_META, row_off=0, name=f'mlstm_meta{tag}')
    ya_main, c_p, n_p, m_p = _mlstm_call(
        pa_m, pc_m, *gate_args, c, n, m,
        n_seq=bp, n_chunks=n_chunks, length=CHUNK, row_off=0, name=f'mlstm_main{tag}')
    ya_samp, c_so, n_so, m_so = _mlstm_call(
        pa_t, pc_t, *gate_args, c_s, n_s, _lane_pad(m_s),
        n_seq=bs, n_chunks=1, length=t_s, row_off=meta_rows, name=f'mlstm_samp{tag}')

    yb_meta, s = _hgrn_call(pb_t, p['lb'], p['hgrn_norm'], zeros(bp, HGRN_H, HGRN_DK, HGRN_DV),
                            n_seq=bp, n_chunks=1, length=N_META, row_off=0, name=f'hgrn_meta{tag}')
    yb_main, sh_p = _hgrn_call(pb_m, p['lb'], p['hgrn_norm'], s,
                               n_seq=bp, n_chunks=n_chunks, length=CHUNK, row_off=0,
                               name=f'hgrn_main{tag}')
    yb_samp, sh_so = _hgrn_call(pb_t, p['lb'], p['hgrn_norm'], sh_s,
                                n_seq=bs, n_chunks=1, length=t_s, row_off=meta_rows,
                                name=f'hgrn_samp{tag}')

    yc_meta, s, shift = _rwkv_call(pc_t, p['rwkv'], zeros(bp, RWKV_H, RWKV_DH, RWKV_DH),
                                   zeros(bp, 1, C_COLS), n_seq=bp, n_chunks=1, length=N_META,
                                   row_off=0, name=f'rwkv_meta{tag}')
    yc_main, sr_p, shift_p = _rwkv_call(pc_m, p['rwkv'], s, shift, n_seq=bp, n_chunks=n_chunks,
                                        length=CHUNK, row_off=0, name=f'rwkv_main{tag}')
    yc_samp, sr_so, shift_so = _rwkv_call(pc_t, p['rwkv'], sr_s, shift_s, n_seq=bs, n_chunks=1,
                                          length=t_s, row_off=meta_rows, name=f'rwkv_samp{tag}')

    y_main = (ya_main, yb_main, yc_main)
    y_tail = tuple(jnp.concatenate([a, b], axis=0)
                   for a, b in ((ya_meta, ya_samp), (yb_meta, yb_samp), (yc_meta, yc_samp)))
    st_p_out = (c_p, n_p, m_p[:, 0, 0:MLSTM_H], sh_p, sr_p, shift_p)
    st_s_out = (c_so, n_so, m_so[:, 0, 0:MLSTM_H], sh_so, sr_so, shift_so)
    return y_main, y_tail, st_p_out, st_s_out


def kernel(x_prompt, x_sample, state_mlstm_C, state_mlstm_n, state_mlstm_m, state_hgrn_S,
           state_rwkv_S, cache_rwkv_shift, meta_tokens, norm_pre, norm_post, w_in,
           mlstm_b_i, mlstm_b_f, mlstm_norm, hgrn_lb_logits, hgrn_norm, rwkv_mu, rwkv_w0,
           rwkv_w_up, rwkv_a0, rwkv_a_up, rwkv_k_k, rwkv_k_a, rwkv_r_k, rwkv_gn_g, rwkv_gn_b,
           w_proj_a, w_proj_b, w_proj_c, w_out):
    bp, t_p, _ = x_prompt.shape
    bs, t_s, _ = x_sample.shape
    depth = w_in.shape[0]
    assert t_p % CHUNK == 0 and t_s % HGRN_SUB == 0 and (bp * N_META) % (RWKV_GROUP * t_s) == 0

    sm = jax.nn.softmax(hgrn_lb_logits.astype(F32), axis=0)
    lb_all = jnp.cumsum(sm, axis=0) - sm[0]

    x_main = x_prompt.reshape(bp * t_p, D_MODEL)
    meta = jnp.broadcast_to(meta_tokens.astype(F32)[None], (bp, N_META, D_MODEL))
    x_tail = jnp.concatenate([meta.reshape(bp * N_META, D_MODEL),
                              x_sample.reshape(bs * t_s, D_MODEL)], axis=0)

    outs_p, outs_s = [], []
    for l in range(depth):
        p = _prep_layer(l, lb_all, norm_pre, norm_post, w_in, mlstm_b_i, mlstm_b_f, mlstm_norm,
                        hgrn_norm, rwkv_mu, rwkv_w0, rwkv_w_up, rwkv_a0, rwkv_a_up, rwkv_k_k,
                        rwkv_k_a, rwkv_r_k, rwkv_gn_g, rwkv_gn_b, w_proj_a, w_proj_b, w_proj_c,
                        w_out)
        proj_main = tuple(_proj(x_main, p['norm_pre'], p[k], f'proj_{k}_main{l}')
                          for k in ('w_a', 'w_b', 'w_c'))
        proj_tail = tuple(_proj(x_tail, p['norm_pre'], p[k], f'proj_{k}_tail{l}')
                          for k in ('w_a', 'w_b', 'w_c'))
        st_s = (state_mlstm_C[l], state_mlstm_n[l], state_mlstm_m[l], state_hgrn_S[l],
                state_rwkv_S[l], cache_rwkv_shift[l])
        y_main, y_tail, st_p_out, st_s_out = _branches(
            p, proj_main, proj_tail, st_s, bp=bp, t_p=t_p, bs=bs, t_s=t_s, tag=str(l))
        x_main = _merge(x_main, p['norm_pre'], *y_main, p['w_g'], p['wpa'], p['wpb'], p['wpc'],
                        p['wo'], p['norm_post'], f'merge_main{l}')
        x_tail = _merge(x_tail, p['norm_pre'], *y_tail, p['w_g'], p['wpa'], p['wpb'], p['wpc'],
                        p['wo'], p['norm_post'], f'merge_tail{l}')
        outs_p.append(st_p_out)
        outs_s.append(st_s_out)

    states_p = tuple(jnp.stack([o[j] for o in outs_p]) for j in range(6))
    states_s = tuple(jnp.stack([o[j] for o in outs_s]) for j in range(6))
    y_prompt = x_main.reshape(bp, t_p, D_MODEL)
    y_sample = x_tail[bp * N_META:].reshape(bs, t_s, D_MODEL)
    return (y_prompt, y_sample) + states_p + states_s
```

```python
import functools
import math

import jax
import jax.numpy as jnp
from jax import lax
from jax.experimental import pallas as pl
from jax.experimental.pallas import tpu as pltpu

F32 = jnp.float32
BF16 = jnp.bfloat16

D_MODEL = 2048
CHUNK = 64
N_META = 16
EPS = 1e-6

MLSTM_H = 8
MLSTM_DH = 128
MLSTM_W = MLSTM_H * MLSTM_DH
HGRN_H = 4
HGRN_DK = 128
HGRN_DV = 128
HGRN_W = HGRN_H * HGRN_DV
RWKV_H = 8
RWKV_DH = 64
RWKV_W = RWKV_H * RWKV_DH
RWKV_RANK = 64
RWKV_GN_EPS = 64e-5
C_COLS = 4 * RWKV_W + 2 * RWKV_RANK

LANES = 128
HGRN_SUB = 8
LOG2E = 1.4426950408889634
A_W = 5 * MLSTM_W
B_W = 4 * HGRN_W
C_W = C_COLS + 3 * LANES
IF_BLOCK = C_COLS // LANES
MERGE_NC = 512
VMEM_LIMIT = 56 * 1024 * 1024


def _dot(a, b):
    return jnp.dot(a, b, preferred_element_type=F32)


def _dot_nt(a, b):
    return lax.dot_general(a, b, (((1,), (1,)), ((), ())), preferred_element_type=F32)


def _dot_tn(a, b):
    return lax.dot_general(a, b, (((0,), (0,)), ((), ())), preferred_element_type=F32)


def _bf(a):
    return a.astype(BF16)


def _sigmoid(x):
    return 1.0 / (1.0 + jnp.exp(-x))


def _silu(x):
    return x * _sigmoid(x)


def _softplus(x):
    return jnp.maximum(x, 0.0) + jnp.log1p(jnp.exp(-jnp.abs(x)))


def _tri(length, strict=False):
    row = lax.broadcasted_iota(jnp.int32, (length, length), 0)
    col = lax.broadcasted_iota(jnp.int32, (length, length), 1)
    return (row > col) if strict else (row >= col)


def _cumsum_time(x, tri_b):
    hi = _bf(x)
    r1 = x - hi.astype(F32)
    mid = _bf(r1)
    lo = _bf(r1 - mid.astype(F32))
    return _dot(tri_b, hi) + _dot(tri_b, mid) + _dot(tri_b, lo)


def _rmsnorm(x, g):
    return x * lax.rsqrt(jnp.mean(x * x, axis=-1, keepdims=True) + EPS) * g


def _row_tile(rows, cap):
    best = None
    for t in range(16, min(rows, cap) + 1, 16):
        if rows % t == 0:
            best = t
    assert best is not None, rows
    return best


def _col_tile(cols, cap):
    best = None
    for t in range(LANES, min(cols, cap) + 1, LANES):
        if cols % t == 0:
            best = t
    assert best is not None, cols
    return best


def _proj_kernel(x_ref, g_ref, w_ref, o_ref):
    h = _bf(_rmsnorm(x_ref[...], g_ref[...]))
    o_ref[...] = _dot(h, w_ref[...])


def _proj(x, g, w, name):
    rows, _ = x.shape
    cols = w.shape[1]
    tm = _row_tile(rows, 1024)
    tn = _col_tile(cols, 1280)
    return pl.pallas_call(
        _proj_kernel,
        grid=(cols // tn, rows // tm),
        in_specs=[
            pl.BlockSpec((tm, D_MODEL), lambda j, i: (i, 0)),
            pl.BlockSpec((1, D_MODEL), lambda j, i: (0, 0)),
            pl.BlockSpec((D_MODEL, tn), lambda j, i: (0, j)),
        ],
        out_specs=pl.BlockSpec((tm, tn), lambda j, i: (i, j)),
        out_shape=jax.ShapeDtypeStruct((rows, cols), F32),
        compiler_params=pltpu.CompilerParams(
            dimension_semantics=("parallel", "parallel"), vmem_limit_bytes=VMEM_LIMIT),
        name=name,
    )(x, g, w)


def _merge_kernel(x_ref, gpre_ref, ya_ref, yb_ref, yc_ref, wg_ref,
                  wpa_ref, wpb_ref, wpc_ref, wo_ref, gpost_ref, o_ref, h_scr, acc_scr):
    j = pl.program_id(1)
    nc = MERGE_NC

    @pl.when(j == 0)
    def _():
        h_scr[...] = _bf(_rmsnorm(x_ref[...], gpre_ref[...]))
        acc_scr[...] = jnp.zeros_like(acc_scr)

    gates = _sigmoid(_dot(h_scr[...], wg_ref[...]))
    merged = (gates[:, 0:nc] * _dot(ya_ref[...], wpa_ref[...])
              + gates[:, nc:2 * nc] * _dot(yb_ref[...], wpb_ref[...])
              + gates[:, 2 * nc:3 * nc] * _dot(yc_ref[...], wpc_ref[...]))
    acc_scr[...] += _dot(_bf(merged), wo_ref[...])

    @pl.when(j == pl.num_programs(1) - 1)
    def _():
        o_ref[...] = x_ref[...] + _rmsnorm(acc_scr[...], gpost_ref[...])


def _merge(x, gpre, ya, yb, yc, wg, wpa, wpb, wpc, wo, gpost, name):
    rows = x.shape[0]
    tm = _row_tile(rows, 640)
    nc = MERGE_NC
    n_chunk = D_MODEL // nc
    row = lambda i, j: (i, 0)
    return pl.pallas_call(
        _merge_kernel,
        grid=(rows // tm, n_chunk),
        in_specs=[
            pl.BlockSpec((tm, D_MODEL), row),
            pl.BlockSpec((1, D_MODEL), lambda i, j: (0, 0)),
            pl.BlockSpec((tm, MLSTM_W), row),
            pl.BlockSpec((tm, HGRN_W), row),
            pl.BlockSpec((tm, RWKV_W), row),
            pl.BlockSpec((D_MODEL, 3 * nc), lambda i, j: (0, j)),
            pl.BlockSpec((MLSTM_W, nc), lambda i, j: (0, j)),
            pl.BlockSpec((HGRN_W, nc), lambda i, j: (0, j)),
            pl.BlockSpec((RWKV_W, nc), lambda i, j: (0, j)),
            pl.BlockSpec((nc, D_MODEL), lambda i, j: (j, 0)),
            pl.BlockSpec((1, D_MODEL), lambda i, j: (0, 0)),
        ],
        out_specs=pl.BlockSpec((tm, D_MODEL), row),
        out_shape=jax.ShapeDtypeStruct((rows, D_MODEL), F32),
        scratch_shapes=[pltpu.VMEM((tm, D_MODEL), BF16), pltpu.VMEM((tm, D_MODEL), F32)],
        compiler_params=pltpu.CompilerParams(
            dimension_semantics=("parallel", "arbitrary"), vmem_limit_bytes=VMEM_LIMIT),
        name=name,
    )(x, gpre, ya, yb, yc, wg, wpa, wpb, wpc, wo, gpost)


def _lane_mean(x, j_b):
    n = x.shape[0]
    hi = _bf(x)
    mid = _bf(x - hi.astype(F32))
    out = _dot(jnp.concatenate([hi, mid], axis=0), j_b)
    return out[0:n] + out[n:]


def _mlstm_stages(q_ref, k_ref, v_ref, o_ref, z_ref, i_ref, f_ref, bias_i_ref, bias_f_ref, norm_ref,
                  y_ref, cn_scr, m_scr, stores, *, length):
    dh = MLSTM_DH
    heads = range(MLSTM_H)
    sls = [slice(h * dh, (h + 1) * dh) for h in heads]
    cns = [cn_scr[h] for h in heads]
    m_prev = m_scr[...]

    causal = _tri(length)
    tri_b = _bf(jnp.where(causal, 1.0, 0.0))
    ig = i_ref[...] + bias_i_ref[...]
    fg = f_ref[...] + bias_f_ref[...]
    log_f = jnp.minimum(fg, 0.0) - jnp.log1p(jnp.exp(-jnp.abs(fg)))
    b_all = _cumsum_time(log_f, tri_b)
    yield
    c_all = ig - b_all
    row = lax.broadcasted_iota(jnp.int32, (length, LANES), 0)
    run_max = c_all
    shift = 1
    while shift < length:
        run_max = jnp.maximum(
            run_max, jnp.where(row >= shift, pltpu.roll(run_max, shift, 0), -jnp.inf))
        shift *= 2
    mx_all = jnp.maximum(run_max, m_prev)
    mt_all = b_all + mx_all
    mx_end = mx_all[length - 1:length]
    m_new = b_all[length - 1:length] + mx_end
    w_old_all = jnp.exp(m_prev - mx_end)
    c_t = c_all.T
    ones = jnp.ones((length, dh), F32)
    j_b = jnp.full((dh, dh), 1.0 / dh, BF16)
    scale = MLSTM_DH ** -0.5
    yield
    k_ts = [(k_ref[:, sl] * scale).T for sl in sls]
    qbs = [_bf(q_ref[:, sl]) for sl in sls]
    s_raw = [_dot(qbs[h], _bf(k_ts[h])) for h in heads]
    yield
    inter = [_dot(qbs[h], _bf(cns[h])) for h in heads]
    yield
    mx_bs = [jnp.broadcast_to(mx_all[:, h:h + 1], (length, dh)) for h in heads]
    es = [jnp.exp(jnp.where(causal, c_t[h:h + 1, :] - mx_bs[h][:, 0:length], -jnp.inf))
          for h in heads]
    yield
    v1s = [_bf(jnp.concatenate([v_ref[:, sl], ones], axis=1)) for sl in sls]
    tots = []
    for h in heads:
        w_inter = jnp.exp(m_prev[:, h:h + 1] - mx_bs[h])
        tots.append(jnp.concatenate([w_inter, w_inter], axis=1) * inter[h]
                    + _dot(_bf(s_raw[h] * es[h]), v1s[h]))
    yield
    new_cn = [w_old_all[:, h:h + 1] * cns[h]
              + _dot(_bf(k_ts[h] * es[h][length - 1:length, :]), v1s[h]) for h in heads]
    yield
    cens = [tots[h][:, 0:dh] - _lane_mean(tots[h][:, 0:dh], j_b) for h in heads]
    yield
    vars_ = [_lane_mean(cens[h] * cens[h], j_b) for h in heads]
    yield
    outs = []
    for h in heads:
        mt_b = jnp.broadcast_to(mt_all[:, h:h + 1], (length, dh))
        inv = 1.0 / jnp.maximum(jnp.abs(tots[h][:, dh:]), jnp.exp(-mt_b))
        hid = cens[h] * inv * lax.rsqrt(vars_[h] * inv * inv + EPS) * norm_ref[:, sls[h]]
        outs.append(_bf(hid * _sigmoid(o_ref[:, sls[h]]) * _silu(z_ref[:, sls[h]])))

    def store():
        y_ref[...] = jnp.concatenate(outs, axis=1)
        for h in heads:
            cn_scr[h] = new_cn[h]
        m_scr[...] = m_new

    stores.append(store)


def _hgrn_stages(q_ref, f_ref, i_ref, z_ref, lb_ref, norm_ref, y_ref, st_scr, stores, *, length):
    dk = HGRN_DK
    sub = min(HGRN_SUB, length)
    n_sub = length // sub
    heads = range(HGRN_H)
    sls = [slice(h * dk, (h + 1) * dk) for h in heads]
    sts = [st_scr[h] for h in heads]

    lb = lb_ref[...]
    fp = f_ref[...]
    f_gate = lb + (1.0 - lb) * _sigmoid(fp)
    k_all = (1.0 - lb) * _sigmoid(-fp)
    tri_b = _bf(jnp.where(_tri(length), 1.0, 0.0))
    a_all = _cumsum_time(jnp.log(f_gate), tri_b)
    lane_l = lax.broadcasted_iota(jnp.int32, (sub, length), 1)
    row_l = lax.broadcasted_iota(jnp.int32, (sub, length), 0)
    yield
    qs = [q_ref[:, sl] for sl in sls]
    ks = [k_all[:, sl] for sl in sls]
    avs = [a_all[:, sl] for sl in sls]
    a2_all = a_all * LOG2E
    a2s = [a2_all[:, sl] for sl in sls]
    vbs = [_bf(i_ref[:, sl]) for sl in sls]
    o_inter = [_dot_nt(_bf(qs[h] * jnp.exp(avs[h])), _bf(sts[h])) for h in heads]
    yield
    new_st = []
    for h in heads:
        a_end = avs[h][length - 1:length]
        new_st.append(jnp.exp(a_end) * sts[h]
                      + _dot_tn(vbs[h], _bf(ks[h] * jnp.exp(a_end - avs[h]))))
    yield

    blocks = [[] for _ in heads]
    for i in range(n_sub):
        r0 = i * sub
        blks = []
        for h in heads:
            if i > 0:
                ref_row = avs[h][r0:r0 + 1]
                q_s = _bf(qs[h][r0:r0 + sub] * jnp.exp(avs[h][r0:r0 + sub] - ref_row))
                k_s = jnp.concatenate(
                    [ks[h][0:r0] * jnp.exp(ref_row - avs[h][0:r0]),
                     jnp.zeros((length - r0, dk), F32)], axis=0)
                blks.append(_dot_nt(q_s, _bf(k_s)))
            else:
                blks.append(jnp.zeros((sub, length), F32))
        for s_idx in range(sub):
            for h in heads:
                a_i = a2s[h][r0:r0 + sub]
                e = jnp.exp2(jnp.minimum(a_i - a_i[s_idx:s_idx + 1], 0.0))
                col_v = jnp.sum(qs[h][r0:r0 + sub] * e * ks[h][r0 + s_idx:r0 + s_idx + 1],
                                axis=-1, keepdims=True)
                blks[h] = jnp.where(lane_l == r0 + s_idx, col_v, blks[h])
            if s_idx % 4 == 3:
                yield
        for h in heads:
            blocks[h].append(jnp.where(lane_l <= r0 + row_l, blks[h], 0.0))

    outs = []
    for h in heads:
        scores = jnp.concatenate(blocks[h], axis=0) if n_sub > 1 else blocks[h][0]
        o = o_inter[h] + _dot(_bf(scores), vbs[h])
        o = o * lax.rsqrt(jnp.mean(o * o, axis=-1, keepdims=True) + EPS)
        outs.append(_bf(o * norm_ref[:, sls[h]] * _silu(z_ref[:, sls[h]])))

    def store():
        y_ref[...] = jnp.concatenate(outs, axis=1)
        for h in heads:
            st_scr[h] = new_st[h]

    stores.append(store)


def _pair_sum(x, even):
    s_even = jnp.sum(jnp.where(even, x, 0.0), axis=-1, keepdims=True)
    s_odd = jnp.sum(jnp.where(even, 0.0, x), axis=-1, keepdims=True)
    return jnp.where(even, s_even, s_odd)


def _rwkv_stages(pc_ref, mu_ref, wa_up_ref, w0_ref, a0_ref, kk_ref, ka_ref, rk_ref, gg_ref, gb_ref,
                 y_ref, bd_scr, carry_scr, stores, *, length):
    dh = RWKV_DH
    w = RWKV_W
    pairs = range(RWKV_H // 2)
    sls = [slice(p * LANES, (p + 1) * LANES) for p in pairs]
    l2 = 2 * length
    bds = [bd_scr[p] for p in pairs]
    gn_g = gg_ref[...]
    gn_b = gb_ref[...]

    pc = pc_ref[...]
    row0 = lax.broadcasted_iota(jnp.int32, (length, C_COLS), 0) == 0
    prev = jnp.where(row0, carry_scr[...], pltpu.roll(pc, 1, 0))
    xs = pc + mu_ref[...] * (prev - pc)
    r = xs[:, 0:w]
    k = xs[:, w:2 * w]
    v = xs[:, 2 * w:3 * w]
    low = xs[:, 3 * w:3 * w + 2 * RWKV_RANK]
    z = xs[:, 3 * w + 2 * RWKV_RANK:]
    lane = lax.broadcasted_iota(jnp.int32, (length, 2 * RWKV_RANK), 1)
    low = jnp.where(lane < RWKV_RANK, jnp.tanh(low), low)
    up = _dot(_bf(low), wa_up_ref[...])
    yield
    w_logit = -_softplus(-(w0_ref[...] + up[:, 0:w])) - 0.5
    log_w = -jnp.exp(w_logit)
    a = _sigmoid(a0_ref[...] + up[:, w:2 * w])
    kk_raw = k * kk_ref[...]
    k2 = k * (1.0 + (a - 1.0) * ka_ref[...])
    rk_bonus = r * k2 * rk_ref[...]
    tri_b = _bf(jnp.where(_tri(length), 1.0, 0.0))
    cum = _cumsum_time(log_w, tri_b)
    yield
    p_in = jnp.exp(cum)
    p_ex = jnp.exp(cum - log_w)
    p_inv = jnp.exp(-cum)
    cum_end = cum[length - 1:length]
    p_end = jnp.exp(cum_end - cum)
    dec_end = jnp.exp(cum_end)

    even = lax.broadcasted_iota(jnp.int32, (length, LANES), 1) < dh
    rows2 = lax.broadcasted_iota(jnp.int32, (l2, l2), 0)
    cols2 = lax.broadcasted_iota(jnp.int32, (l2, l2), 1)
    same = (rows2 >= length) == (cols2 >= length)
    strict_bd = same & (rows2 > cols2)
    incl_bd = same & (rows2 >= cols2)
    rows_s = lax.broadcasted_iota(jnp.int32, (LANES, LANES), 0)
    cols_s = lax.broadcasted_iota(jnp.int32, (LANES, LANES), 1)
    state_bd = (rows_s >= dh) == (cols_s >= dh)
    n_double = int(math.log2(length))
    assert 2 ** n_double == length

    def stack(x):
        return jnp.concatenate([jnp.where(even, x, 0.0), jnp.where(even, 0.0, x)], axis=0)

    kas, lhss, grams = [], [], []
    for p in pairs:
        sl = sls[p]
        kk_r = kk_raw[:, sl]
        kk = kk_r / jnp.maximum(jnp.sqrt(_pair_sum(kk_r * kk_r, even)), 1e-12)
        ka = kk * a[:, sl]
        lhs = _bf(jnp.concatenate([stack(-kk * p_ex[:, sl]), stack(r[:, sl] * p_in[:, sl])], axis=0))
        bh = _bf(ka * p_inv[:, sl])
        kh = _bf(k2[:, sl] * p_inv[:, sl])
        rhs = jnp.concatenate([bh, bh, kh, kh], axis=0)
        grams.append(_dot_nt(lhs, rhs))
        kas.append(ka)
        lhss.append(lhs)
    yield
    uy0s = [_dot_nt(lhss[p], _bf(bds[p])) for p in pairs]
    v_stacks = [_bf(stack(v[:, sl])) for sl in sls]
    yield
    n_mats = [jnp.where(strict_bd, grams[p][0:l2, 0:l2], 0.0) for p in pairs]
    xs_u = [uy0s[p][0:l2] + _dot(_bf(jnp.where(strict_bd, grams[p][0:l2, l2:], 0.0)), v_stacks[p])
            for p in pairs]
    abks = [_bf(jnp.concatenate([jnp.where(incl_bd, grams[p][l2:, 0:l2], 0.0),
                                 jnp.where(incl_bd, grams[p][l2:, l2:], 0.0)], axis=1))
            for p in pairs]
    yield
    for step in range(n_double):
        for p in pairs:
            n_b = _bf(n_mats[p])
            xs_u[p] = xs_u[p] + _dot(n_b, _bf(xs_u[p]))
            if step + 1 < n_double:
                n_mats[p] = _dot(n_b, n_b)
        yield

    new_bd, ys = [], []
    for p in pairs:
        sl = sls[p]
        u_st = xs_u[p]
        y_st = uy0s[p][l2:] + _dot(abks[p], jnp.concatenate([_bf(u_st), v_stacks[p]], axis=0))
        ys.append(y_st[0:length] + y_st[length:])
        u_pair = u_st[0:length] + u_st[length:]
        upd = _dot_tn(_bf(jnp.concatenate([u_pair, v[:, sl]], axis=0)),
                      _bf(jnp.concatenate([kas[p] * p_end[:, sl], k2[:, sl] * p_end[:, sl]], axis=0)))
        new_bd.append(bds[p] * dec_end[:, sl] + jnp.where(state_bd, upd, 0.0))
    yield
    outs = []
    for p in pairs:
        sl = sls[p]
        mu = _pair_sum(ys[p], even) * (1.0 / dh)
        cen = ys[p] - mu
        var = _pair_sum(cen * cen, even) * (1.0 / dh)
        y = cen * lax.rsqrt(var + RWKV_GN_EPS) * gn_g[:, sl] + gn_b[:, sl]
        bonus = _pair_sum(rk_bonus[:, sl], even) * v[:, sl]
        outs.append(_bf((y + bonus) * _silu(z[:, sl])))

    def store():
        y_ref[...] = jnp.concatenate(outs, axis=1)
        carry_scr[...] = pc[length - 1:length, :]
        for p in pairs:
            bd_scr[p] = new_bd[p]

    stores.append(store)


def _branch_kernel(q_ref, k_ref, v_ref, o_ref, z_ref, ig_ref, fg_ref, bias_i_ref, bias_f_ref,
                   anorm_ref, hq_ref, hf_ref, hi_ref, hz_ref, lb_ref, hnorm_ref,
                   pc_ref, mu_ref, wa_up_ref, w0_ref, a0_ref, kk_ref, ka_ref, rk_ref, gg_ref, gb_ref,
                   c0_ref, n0_ref, m0_ref, hs0_ref, rs0_ref, shift0_ref,
                   ya_ref, yb_ref, yc_ref, c_out, n_out, m_out, hs_out, rs_out, shift_out,
                   cn_scr, m_scr, st_scr, bd_scr, carry_scr, *, length):
    c_idx = pl.program_id(1)
    dh = MLSTM_DH
    rh = RWKV_DH

    @pl.when(c_idx == 0)
    def _():
        n_t = n0_ref[0].T
        for h in range(MLSTM_H):
            cn_scr[h, :, 0:dh] = c0_ref[0, h].T
            cn_scr[h, :, dh:2 * dh] = jnp.broadcast_to(n_t[:, h:h + 1], (dh, dh))
        m_scr[...] = m0_ref[0]
        for h in range(HGRN_H):
            st_scr[h] = hs0_ref[0, h].T
        zero = jnp.zeros((rh, rh), F32)
        for p in range(RWKV_H // 2):
            top = jnp.concatenate([rs0_ref[0, 2 * p], zero], axis=1)
            bot = jnp.concatenate([zero, rs0_ref[0, 2 * p + 1]], axis=1)
            bd_scr[p] = jnp.concatenate([top, bot], axis=0)
        carry_scr[...] = shift0_ref[0]

    stores = []
    active = [
        _rwkv_stages(pc_ref, mu_ref, wa_up_ref, w0_ref, a0_ref, kk_ref, ka_ref, rk_ref, gg_ref,
                     gb_ref, yc_ref, bd_scr, carry_scr, stores, length=length),
        _hgrn_stages(hq_ref, hf_ref, hi_ref, hz_ref, lb_ref, hnorm_ref, yb_ref, st_scr, stores,
                     length=length),
        _mlstm_stages(q_ref, k_ref, v_ref, o_ref, z_ref, ig_ref, fg_ref, bias_i_ref, bias_f_ref,
                      anorm_ref, ya_ref, cn_scr, m_scr, stores, length=length),
    ]
    while active:
        for gen in list(active):
            if next(gen, StopIteration) is StopIteration:
                active.remove(gen)
    for store in stores:
        store()

    @pl.when(c_idx == pl.num_programs(1) - 1)
    def _():
        for h in range(MLSTM_H):
            c_out[0, h] = cn_scr[h, :, 0:dh].T
            n_out[0, h:h + 1, :] = cn_scr[h, :, dh:2 * dh].T[0:1, :]
        m_out[0] = m_scr[...]
        for h in range(HGRN_H):
            hs_out[0, h] = st_scr[h].T
        for p in range(RWKV_H // 2):
            rs_out[0, 2 * p] = bd_scr[p, 0:rh, 0:rh]
            rs_out[0, 2 * p + 1] = bd_scr[p, rh:2 * rh, rh:2 * rh]
        shift_out[0] = carry_scr[...]


def _branch_call(pa, pb, pc, p, states, *, n_seq, n_chunks, length, row_off, name):
    assert row_off % length == 0
    blk0 = row_off // length
    rows = n_seq * n_chunks * length

    def col(j):
        return lambda b, c: (blk0 + b * n_chunks + c, j)

    def const(shape):
        return pl.BlockSpec(shape, lambda b, c: (0,) * len(shape))

    def per_seq(*tail):
        return pl.BlockSpec((1,) + tail, lambda b, c: (b,) + (0,) * len(tail))

    out_rows = lambda b, c: (b * n_chunks + c, 0)
    rw = p['rwkv']
    state_specs = [
        per_seq(MLSTM_H, MLSTM_DH, MLSTM_DH), per_seq(MLSTM_H, MLSTM_DH), per_seq(1, LANES),
        per_seq(HGRN_H, HGRN_DK, HGRN_DV), per_seq(RWKV_H, RWKV_DH, RWKV_DH), per_seq(1, C_COLS)]
    state_shapes = [
        jax.ShapeDtypeStruct((n_seq, MLSTM_H, MLSTM_DH, MLSTM_DH), F32),
        jax.ShapeDtypeStruct((n_seq, MLSTM_H, MLSTM_DH), F32),
        jax.ShapeDtypeStruct((n_seq, 1, LANES), F32),
        jax.ShapeDtypeStruct((n_seq, HGRN_H, HGRN_DK, HGRN_DV), F32),
        jax.ShapeDtypeStruct((n_seq, RWKV_H, RWKV_DH, RWKV_DH), F32),
        jax.ShapeDtypeStruct((n_seq, 1, C_COLS), F32)]
    outs = pl.pallas_call(
        functools.partial(_branch_kernel, length=length),
        grid=(n_seq, n_chunks),
        in_specs=[pl.BlockSpec((length, MLSTM_W), col(j)) for j in range(5)]
        + [pl.BlockSpec((length, LANES), col(IF_BLOCK)),
           pl.BlockSpec((length, LANES), col(IF_BLOCK + 1)),
           const((1, LANES)), const((1, LANES)), const((1, MLSTM_W))]
        + [pl.BlockSpec((length, HGRN_W), col(j)) for j in range(4)]
        + [const((1, HGRN_W)), const((1, HGRN_W))]
        + [pl.BlockSpec((length, C_COLS), col(0)), const((1, C_COLS)),
           const((2 * RWKV_RANK, 2 * RWKV_W))] + [const((1, RWKV_W))] * 7
        + state_specs,
        out_specs=[pl.BlockSpec((length, MLSTM_W), out_rows),
                   pl.BlockSpec((length, HGRN_W), out_rows),
                   pl.BlockSpec((length, RWKV_W), out_rows)] + state_specs,
        out_shape=[jax.ShapeDtypeStruct((rows, MLSTM_W), BF16),
                   jax.ShapeDtypeStruct((rows, HGRN_W), BF16),
                   jax.ShapeDtypeStruct((rows, RWKV_W), BF16)] + state_shapes,
        scratch_shapes=[pltpu.VMEM((MLSTM_H, MLSTM_DH, 2 * MLSTM_DH), F32),
                        pltpu.VMEM((1, LANES), F32),
                        pltpu.VMEM((HGRN_H, HGRN_DV, HGRN_DK), F32),
                        pltpu.VMEM((RWKV_H // 2, 2 * RWKV_DH, 2 * RWKV_DH), F32),
                        pltpu.VMEM((1, C_COLS), F32)],
        compiler_params=pltpu.CompilerParams(
            dimension_semantics=("parallel", "arbitrary"), vmem_limit_bytes=VMEM_LIMIT),
        name=name,
    )(pa, pa, pa, pa, pa, pc, pc, p['bias_i'], p['bias_f'], p['mlstm_norm'],
      pb, pb, pb, pb, p['lb'], p['hgrn_norm'],
      pc, rw['mu'], rw['wa_up'], rw['w0'], rw['a0'], rw['k_k'], rw['k_a'], rw['r_k'], rw['gn_g'],
      rw['gn_b'], *states)
    return outs[0:3], outs[3:]


def _prep_layer(l, lb_all, norm_pre, norm_post, w_in, mlstm_b_i, mlstm_b_f, mlstm_norm, hgrn_norm,
                rwkv_mu, rwkv_w0, rwkv_w_up, rwkv_a0, rwkv_a_up, rwkv_k_k, rwkv_k_a, rwkv_r_k,
                rwkv_gn_g, rwkv_gn_b, w_proj_a, w_proj_b, w_proj_c, w_out):
    w = w_in[l]
    a_cols = 5 * MLSTM_W + 2 * MLSTM_H
    if0 = 3 * MLSTM_W
    b0 = a_cols
    c0 = b0 + B_W
    g0 = c0 + C_COLS
    w_a = _bf(jnp.concatenate([w[:, 0:if0], w[:, if0 + 2 * MLSTM_H:a_cols]], axis=1))
    w_b = _bf(w[:, b0:c0])
    gate_pad = jnp.zeros((D_MODEL, LANES - MLSTM_H), F32)
    w_c = _bf(jnp.concatenate(
        [w[:, c0:g0], w[:, if0:if0 + MLSTM_H], gate_pad,
         w[:, if0 + MLSTM_H:if0 + 2 * MLSTM_H], gate_pad,
         jnp.zeros((D_MODEL, LANES), F32)], axis=1))
    n_chunk = D_MODEL // MERGE_NC
    w_g = _bf(jnp.concatenate(
        [w[:, g0 + b * D_MODEL + j * MERGE_NC:g0 + b * D_MODEL + (j + 1) * MERGE_NC]
         for j in range(n_chunk) for b in range(3)], axis=1))
    lane_pad = jnp.zeros((LANES - MLSTM_H,), F32)
    bias_i = jnp.concatenate([mlstm_b_i[l].astype(F32), lane_pad]).reshape(1, LANES)
    bias_f = jnp.concatenate([mlstm_b_f[l].astype(F32), lane_pad]).reshape(1, LANES)
    zero = jnp.zeros((RWKV_RANK, RWKV_W), F32)
    wa_up = _bf(jnp.concatenate([jnp.concatenate([rwkv_w_up[l], zero], axis=1),
                                 jnp.concatenate([zero, rwkv_a_up[l]], axis=1)], axis=0))
    row = lambda a: a.reshape(1, -1).astype(F32)
    return dict(
        norm_pre=row(norm_pre[l]), norm_post=row(norm_post[l]), w_a=w_a, w_b=w_b, w_c=w_c, w_g=w_g,
        bias_i=bias_i, bias_f=bias_f, mlstm_norm=row(mlstm_norm[l]), lb=row(lb_all[l]),
        hgrn_norm=row(hgrn_norm[l]),
        rwkv=dict(mu=row(rwkv_mu[l]), wa_up=wa_up, w0=row(rwkv_w0[l]), a0=row(rwkv_a0[l]),
                  k_k=row(rwkv_k_k[l]), k_a=row(rwkv_k_a[l]), r_k=row(rwkv_r_k[l]),
                  gn_g=row(rwkv_gn_g[l]), gn_b=row(rwkv_gn_b[l])),
        wpa=_bf(w_proj_a[l]), wpb=_bf(w_proj_b[l]), wpc=_bf(w_proj_c[l]), wo=_bf(w_out[l]))


def _lane_pad(m):
    return jnp.pad(m.astype(F32), ((0, 0), (0, LANES - m.shape[1])))[:, None, :]


def _branches(p, proj_main, proj_tail, st_s, *, bp, t_p, bs, t_s, tag):
    c_s, n_s, m_s, sh_s, sr_s, shift_s = st_s
    zeros = lambda *s: jnp.zeros(s, F32)
    st_zero = (zeros(bp, MLSTM_H, MLSTM_DH, MLSTM_DH), zeros(bp, MLSTM_H, MLSTM_DH),
               zeros(bp, 1, LANES), zeros(bp, HGRN_H, HGRN_DK, HGRN_DV),
               zeros(bp, RWKV_H, RWKV_DH, RWKV_DH), zeros(bp, 1, C_COLS))
    y_meta, st_meta = _branch_call(*proj_tail, p, st_zero, n_seq=bp, n_chunks=1, length=N_META,
                                   row_off=0, name=f'branch_meta{tag}')
    y_main, st_p = _branch_call(*proj_main, p, st_meta, n_seq=bp, n_chunks=t_p // CHUNK,
                                length=CHUNK, row_off=0, name=f'branch_main{tag}')
    y_samp, st_so = _branch_call(*proj_tail, p, (c_s, n_s, _lane_pad(m_s), sh_s, sr_s, shift_s),
                                 n_seq=bs, n_chunks=1, length=t_s, row_off=bp * N_META,
                                 name=f'branch_samp{tag}')
    y_tail = tuple(jnp.concatenate([a, b], axis=0) for a, b in zip(y_meta, y_samp))
    unpad = lambda st: (st[0], st[1], st[2][:, 0, 0:MLSTM_H]) + tuple(st[3:])
    return y_main, y_tail, unpad(st_p), unpad(st_so)


def kernel(x_prompt, x_sample, state_mlstm_C, state_mlstm_n, state_mlstm_m, state_hgrn_S,
           state_rwkv_S, cache_rwkv_shift, meta_tokens, norm_pre, norm_post, w_in,
           mlstm_b_i, mlstm_b_f, mlstm_norm, hgrn_lb_logits, hgrn_norm, rwkv_mu, rwkv_w0,
           rwkv_w_up, rwkv_a0, rwkv_a_up, rwkv_k_k, rwkv_k_a, rwkv_r_k, rwkv_gn_g, rwkv_gn_b,
           w_proj_a, w_proj_b, w_proj_c, w_out):
    bp, t_p, _ = x_prompt.shape
    bs, t_s, _ = x_sample.shape
    depth = w_in.shape[0]
    assert t_p % CHUNK == 0 and t_s % HGRN_SUB == 0 and (bp * N_META) % t_s == 0

    sm = jax.nn.softmax(hgrn_lb_logits.astype(F32), axis=0)
    lb_all = jnp.cumsum(sm, axis=0) - sm[0]

    x_main = x_prompt.reshape(bp * t_p, D_MODEL)
    meta = jnp.broadcast_to(meta_tokens.astype(F32)[None], (bp, N_META, D_MODEL))
    x_tail = jnp.concatenate([meta.reshape(bp * N_META, D_MODEL),
                              x_sample.reshape(bs * t_s, D_MODEL)], axis=0)

    outs_p, outs_s = [], []
    for l in range(depth):
        p = _prep_layer(l, lb_all, norm_pre, norm_post, w_in, mlstm_b_i, mlstm_b_f, mlstm_norm,
                        hgrn_norm, rwkv_mu, rwkv_w0, rwkv_w_up, rwkv_a0, rwkv_a_up, rwkv_k_k,
                        rwkv_k_a, rwkv_r_k, rwkv_gn_g, rwkv_gn_b, w_proj_a, w_proj_b, w_proj_c,
                        w_out)
        proj_main = tuple(_proj(x_main, p['norm_pre'], p[k], f'proj_{k}_main{l}')
                          for k in ('w_a', 'w_b', 'w_c'))
        proj_tail = tuple(_proj(x_tail, p['norm_pre'], p[k], f'proj_{k}_tail{l}')
                          for k in ('w_a', 'w_b', 'w_c'))
        st_s = (state_mlstm_C[l], state_mlstm_n[l], state_mlstm_m[l], state_hgrn_S[l],
                state_rwkv_S[l], cache_rwkv_shift[l])
        y_main, y_tail, st_p_out, st_s_out = _branches(
            p, proj_main, proj_tail, st_s, bp=bp, t_p=t_p, bs=bs, t_s=t_s, tag=str(l))
        x_main = _merge(x_main, p['norm_pre'], *y_main, p['w_g'], p['wpa'], p['wpb'], p['wpc'],
                        p['wo'], p['norm_post'], f'merge_main{l}')
        x_tail = _merge(x_tail, p['norm_pre'], *y_tail, p['w_g'], p['wpa'], p['wpb'], p['wpc'],
                        p['wo'], p['norm_post'], f'merge_tail{l}')
        outs_p.append(st_p_out)
        outs_s.append(st_s_out)

    states_p = tuple(jnp.stack([o[j] for o in outs_p]) for j in range(6))
    states_s = tuple(jnp.stack([o[j] for o in outs_s]) for j in range(6))
    y_prompt = x_main.reshape(bp, t_p, D_MODEL)
    y_sample = x_tail[bp * N_META:].reshape(bs, t_s, D_MODEL)
    return (y_prompt, y_sample) + states_p + states_s
```

```python
import functools
import math

import jax
import jax.numpy as jnp
from jax import lax
from jax.experimental import pallas as pl
from jax.experimental.pallas import tpu as pltpu

F32 = jnp.float32
BF16 = jnp.bfloat16

D_MODEL = 2048
CHUNK = 64
N_META = 16
EPS = 1e-6

MLSTM_H = 8
MLSTM_DH = 128
MLSTM_W = MLSTM_H * MLSTM_DH
HGRN_H = 4
HGRN_DK = 128
HGRN_DV = 128
HGRN_W = HGRN_H * HGRN_DV
RWKV_H = 8
RWKV_DH = 64
RWKV_W = RWKV_H * RWKV_DH
RWKV_RANK = 64
RWKV_GN_EPS = 64e-5
C_COLS = 4 * RWKV_W + 2 * RWKV_RANK

LANES = 128
HGRN_SUB = 8
LOG2E = 1.4426950408889634
A_W = 5 * MLSTM_W
B_W = 4 * HGRN_W
C_W = C_COLS + 3 * LANES
IF_BLOCK = C_COLS // LANES
MERGE_NC = 512
VMEM_LIMIT = 56 * 1024 * 1024


def _dot(a, b):
    return jnp.dot(a, b, preferred_element_type=F32)


def _dot_nt(a, b):
    return lax.dot_general(a, b, (((1,), (1,)), ((), ())), preferred_element_type=F32)


def _dot_tn(a, b):
    return lax.dot_general(a, b, (((0,), (0,)), ((), ())), preferred_element_type=F32)


def _bf(a):
    return a.astype(BF16)


def _sigmoid(x):
    return 1.0 / (1.0 + jnp.exp(-x))


def _silu(x):
    return x * _sigmoid(x)


def _softplus(x):
    return jnp.maximum(x, 0.0) + jnp.log1p(jnp.exp(-jnp.abs(x)))


def _tri(length, strict=False):
    row = lax.broadcasted_iota(jnp.int32, (length, length), 0)
    col = lax.broadcasted_iota(jnp.int32, (length, length), 1)
    return (row > col) if strict else (row >= col)


def _cumsum_time(x, tri_b):
    hi = _bf(x)
    r1 = x - hi.astype(F32)
    mid = _bf(r1)
    lo = _bf(r1 - mid.astype(F32))
    return _dot(tri_b, hi) + _dot(tri_b, mid) + _dot(tri_b, lo)


def _rmsnorm(x, g):
    return x * lax.rsqrt(jnp.mean(x * x, axis=-1, keepdims=True) + EPS) * g


def _row_tile(rows, cap):
    best = None
    for t in range(16, min(rows, cap) + 1, 16):
        if rows % t == 0:
            best = t
    assert best is not None, rows
    return best


def _col_tile(cols, cap):
    best = None
    for t in range(LANES, min(cols, cap) + 1, LANES):
        if cols % t == 0:
            best = t
    assert best is not None, cols
    return best


def _proj_kernel(x_ref, g_ref, w_ref, o_ref):
    h = _bf(_rmsnorm(x_ref[...], g_ref[...]))
    o_ref[...] = _dot(h, w_ref[...])


def _layer_spec(arr, layer):
    tail = arr.shape[1:]
    return pl.BlockSpec((None,) + tail, lambda *_: (layer,) + (0,) * len(tail))


def _proj(x, g, layer, w, name):
    rows, _ = x.shape
    cols = w.shape[1]
    tm = _row_tile(rows, 1024)
    tn = _col_tile(cols, 1280)
    return pl.pallas_call(
        _proj_kernel,
        grid=(cols // tn, rows // tm),
        in_specs=[
            pl.BlockSpec((tm, D_MODEL), lambda j, i: (i, 0)),
            _layer_spec(g, layer),
            pl.BlockSpec((D_MODEL, tn), lambda j, i: (0, j)),
        ],
        out_specs=pl.BlockSpec((tm, tn), lambda j, i: (i, j)),
        out_shape=jax.ShapeDtypeStruct((rows, cols), F32),
        compiler_params=pltpu.CompilerParams(
            dimension_semantics=("parallel", "parallel"), vmem_limit_bytes=VMEM_LIMIT),
        name=name,
    )(x, g, w)


def _merge_kernel(x_ref, gpre_ref, ya_ref, yb_ref, yc_ref, wg_ref,
                  wpa_ref, wpb_ref, wpc_ref, wo_ref, gpost_ref, o_ref, h_scr, acc_scr):
    j = pl.program_id(1)
    nc = MERGE_NC

    @pl.when(j == 0)
    def _():
        h_scr[...] = _bf(_rmsnorm(x_ref[...], gpre_ref[...]))
        acc_scr[...] = jnp.zeros_like(acc_scr)

    gates = _sigmoid(_dot(h_scr[...], wg_ref[...]))
    merged = (gates[:, 0:nc] * _dot(ya_ref[...], wpa_ref[...])
              + gates[:, nc:2 * nc] * _dot(yb_ref[...], wpb_ref[...])
              + gates[:, 2 * nc:3 * nc] * _dot(yc_ref[...], wpc_ref[...]))
    acc_scr[...] += _dot(_bf(merged), wo_ref[...])

    @pl.when(j == pl.num_programs(1) - 1)
    def _():
        o_ref[...] = x_ref[...] + _rmsnorm(acc_scr[...], gpost_ref[...])


def _merge(x, gpre, ya, yb, yc, wg, wpa, wpb, wpc, wo, gpost, layer, name):
    rows = x.shape[0]
    tm = _row_tile(rows, 640)
    nc = MERGE_NC
    n_chunk = D_MODEL // nc
    row = lambda i, j: (i, 0)
    return pl.pallas_call(
        _merge_kernel,
        grid=(rows // tm, n_chunk),
        in_specs=[
            pl.BlockSpec((tm, D_MODEL), row),
            _layer_spec(gpre, layer),
            pl.BlockSpec((tm, MLSTM_W), row),
            pl.BlockSpec((tm, HGRN_W), row),
            pl.BlockSpec((tm, RWKV_W), row),
            pl.BlockSpec((D_MODEL, 3 * nc), lambda i, j: (0, j)),
            pl.BlockSpec((None, MLSTM_W, nc), lambda i, j: (layer, 0, j)),
            pl.BlockSpec((None, HGRN_W, nc), lambda i, j: (layer, 0, j)),
            pl.BlockSpec((None, RWKV_W, nc), lambda i, j: (layer, 0, j)),
            pl.BlockSpec((None, nc, D_MODEL), lambda i, j: (layer, j, 0)),
            _layer_spec(gpost, layer),
        ],
        out_specs=pl.BlockSpec((tm, D_MODEL), row),
        out_shape=jax.ShapeDtypeStruct((rows, D_MODEL), F32),
        scratch_shapes=[pltpu.VMEM((tm, D_MODEL), BF16), pltpu.VMEM((tm, D_MODEL), F32)],
        compiler_params=pltpu.CompilerParams(
            dimension_semantics=("parallel", "arbitrary"), vmem_limit_bytes=VMEM_LIMIT),
        name=name,
    )(x, gpre, ya, yb, yc, wg, wpa, wpb, wpc, wo, gpost)


def _lane_mean(x, j_b):
    n = x.shape[0]
    hi = _bf(x)
    mid = _bf(x - hi.astype(F32))
    out = _dot(jnp.concatenate([hi, mid], axis=0), j_b)
    return out[0:n] + out[n:]


def _mlstm_stages(pa_ref, pc_ref, bias_i_ref, bias_f_ref, norm_ref, y_ref, cn_scr, m_scr, stores,
                  *, length):
    dh = MLSTM_DH
    heads = range(MLSTM_H)
    sls = [slice(h * dh, (h + 1) * dh) for h in heads]
    col = lambda part, h: slice(part * MLSTM_W + h * dh, part * MLSTM_W + (h + 1) * dh)
    cns = [cn_scr[h] for h in heads]
    m_prev = m_scr[...]

    causal = _tri(length)
    tri_b = _bf(jnp.where(causal, 1.0, 0.0))
    gate0 = IF_BLOCK * LANES
    ig = pc_ref[:, gate0:gate0 + LANES] + bias_i_ref[...]
    fg = pc_ref[:, gate0 + LANES:gate0 + 2 * LANES] + bias_f_ref[...]
    log_f = jnp.minimum(fg, 0.0) - jnp.log1p(jnp.exp(-jnp.abs(fg)))
    b_all = _cumsum_time(log_f, tri_b)
    yield
    c_all = ig - b_all
    row = lax.broadcasted_iota(jnp.int32, (length, LANES), 0)
    run_max = c_all
    shift = 1
    while shift < length:
        run_max = jnp.maximum(
            run_max, jnp.where(row >= shift, pltpu.roll(run_max, shift, 0), -jnp.inf))
        shift *= 2
    mx_all = jnp.maximum(run_max, m_prev)
    mt_all = b_all + mx_all
    mx_end = mx_all[length - 1:length]
    m_new = b_all[length - 1:length] + mx_end
    w_old_all = jnp.exp(m_prev - mx_end)
    c_t = c_all.T
    ones = jnp.ones((length, dh), F32)
    j_b = jnp.full((dh, dh), 1.0 / dh, BF16)
    scale = MLSTM_DH ** -0.5
    yield
    k_ts = [(pa_ref[:, col(1, h)] * scale).T for h in heads]
    qbs = [_bf(pa_ref[:, col(0, h)]) for h in heads]
    s_raw = [_dot(qbs[h], _bf(k_ts[h])) for h in heads]
    yield
    inter = [_dot(qbs[h], _bf(cns[h])) for h in heads]
    yield
    mx_bs = [jnp.broadcast_to(mx_all[:, h:h + 1], (length, dh)) for h in heads]
    es = [jnp.exp(jnp.where(causal, c_t[h:h + 1, :] - mx_bs[h][:, 0:length], -jnp.inf))
          for h in heads]
    yield
    v1s = [_bf(jnp.concatenate([pa_ref[:, col(2, h)], ones], axis=1)) for h in heads]
    tots = []
    for h in heads:
        w_inter = jnp.exp(m_prev[:, h:h + 1] - mx_bs[h])
        tots.append(jnp.concatenate([w_inter, w_inter], axis=1) * inter[h]
                    + _dot(_bf(s_raw[h] * es[h]), v1s[h]))
    yield
    new_cn = [w_old_all[:, h:h + 1] * cns[h]
              + _dot(_bf(k_ts[h] * es[h][length - 1:length, :]), v1s[h]) for h in heads]
    yield
    cens = [tots[h][:, 0:dh] - _lane_mean(tots[h][:, 0:dh], j_b) for h in heads]
    yield
    vars_ = [_lane_mean(cens[h] * cens[h], j_b) for h in heads]
    yield
    outs = []
    for h in heads:
        mt_b = jnp.broadcast_to(mt_all[:, h:h + 1], (length, dh))
        inv = 1.0 / jnp.maximum(jnp.abs(tots[h][:, dh:]), jnp.exp(-mt_b))
        hid = cens[h] * inv * lax.rsqrt(vars_[h] * inv * inv + EPS) * norm_ref[:, sls[h]]
        outs.append(_bf(hid * _sigmoid(pa_ref[:, col(3, h)]) * _silu(pa_ref[:, col(4, h)])))

    def store():
        y_ref[...] = jnp.concatenate(outs, axis=1)
        for h in heads:
            cn_scr[h] = new_cn[h]
        m_scr[...] = m_new

    stores.append(store)


def _hgrn_stages(pb_ref, lb_ref, norm_ref, y_ref, st_scr, stores, *, length):
    dk = HGRN_DK
    sub = min(HGRN_SUB, length)
    n_sub = length // sub
    heads = range(HGRN_H)
    sls = [slice(h * dk, (h + 1) * dk) for h in heads]
    col = lambda part, h: slice(part * HGRN_W + h * dk, part * HGRN_W + (h + 1) * dk)
    sts = [st_scr[h] for h in heads]

    lb = lb_ref[...]
    fp = pb_ref[:, HGRN_W:2 * HGRN_W]
    f_gate = lb + (1.0 - lb) * _sigmoid(fp)
    k_all = (1.0 - lb) * _sigmoid(-fp)
    tri_b = _bf(jnp.where(_tri(length), 1.0, 0.0))
    a_all = _cumsum_time(jnp.log(f_gate), tri_b)
    lane_l = lax.broadcasted_iota(jnp.int32, (sub, length), 1)
    row_l = lax.broadcasted_iota(jnp.int32, (sub, length), 0)
    yield
    qs = [pb_ref[:, col(0, h)] for h in heads]
    ks = [k_all[:, sl] for sl in sls]
    avs = [a_all[:, sl] for sl in sls]
    a2_all = a_all * LOG2E
    a2s = [a2_all[:, sl] for sl in sls]
    vbs = [_bf(pb_ref[:, col(2, h)]) for h in heads]
    o_inter = [_dot_nt(_bf(qs[h] * jnp.exp(avs[h])), _bf(sts[h])) for h in heads]
    yield
    new_st = []
    for h in heads:
        a_end = avs[h][length - 1:length]
        new_st.append(jnp.exp(a_end) * sts[h]
                      + _dot_tn(vbs[h], _bf(ks[h] * jnp.exp(a_end - avs[h]))))
    yield

    blocks = [[] for _ in heads]
    for i in range(n_sub):
        r0 = i * sub
        blks = []
        for h in heads:
            if i > 0:
                ref_row = avs[h][r0:r0 + 1]
                q_s = _bf(qs[h][r0:r0 + sub] * jnp.exp(avs[h][r0:r0 + sub] - ref_row))
                k_s = jnp.concatenate(
                    [ks[h][0:r0] * jnp.exp(ref_row - avs[h][0:r0]),
                     jnp.zeros((length - r0, dk), F32)], axis=0)
                blks.append(_dot_nt(q_s, _bf(k_s)))
            else:
                blks.append(jnp.zeros((sub, length), F32))
        for s_idx in range(sub):
            for h in heads:
                a_i = a2s[h][r0:r0 + sub]
                e = jnp.exp2(jnp.minimum(a_i - a_i[s_idx:s_idx + 1], 0.0))
                col_v = jnp.sum(qs[h][r0:r0 + sub] * e * ks[h][r0 + s_idx:r0 + s_idx + 1],
                                axis=-1, keepdims=True)
                blks[h] = jnp.where(lane_l == r0 + s_idx, col_v, blks[h])
        yield
        for h in heads:
            blocks[h].append(jnp.where(lane_l <= r0 + row_l, blks[h], 0.0))

    outs = []
    for h in heads:
        scores = jnp.concatenate(blocks[h], axis=0) if n_sub > 1 else blocks[h][0]
        o = o_inter[h] + _dot(_bf(scores), vbs[h])
        o = o * lax.rsqrt(jnp.mean(o * o, axis=-1, keepdims=True) + EPS)
        outs.append(_bf(o * norm_ref[:, sls[h]] * _silu(pb_ref[:, col(3, h)])))

    def store():
        y_ref[...] = jnp.concatenate(outs, axis=1)
        for h in heads:
            st_scr[h] = new_st[h]

    stores.append(store)


def _pair_sum(x, even):
    s_even = jnp.sum(jnp.where(even, x, 0.0), axis=-1, keepdims=True)
    s_odd = jnp.sum(jnp.where(even, 0.0, x), axis=-1, keepdims=True)
    return jnp.where(even, s_even, s_odd)


def _rwkv_stages(pc_ref, mu_ref, wa_up_ref, w0_ref, a0_ref, kk_ref, ka_ref, rk_ref, gg_ref, gb_ref,
                 y_ref, bd_scr, carry_scr, stores, *, length):
    dh = RWKV_DH
    w = RWKV_W
    pairs = range(RWKV_H // 2)
    sls = [slice(p * LANES, (p + 1) * LANES) for p in pairs]
    l2 = 2 * length
    bds = [bd_scr[p] for p in pairs]
    gn_g = gg_ref[...]
    gn_b = gb_ref[...]

    pc = pc_ref[:, 0:C_COLS]
    row0 = lax.broadcasted_iota(jnp.int32, (length, C_COLS), 0) == 0
    prev = jnp.where(row0, carry_scr[...], pltpu.roll(pc, 1, 0))
    xs = pc + mu_ref[...] * (prev - pc)
    r = xs[:, 0:w]
    k = xs[:, w:2 * w]
    v = xs[:, 2 * w:3 * w]
    low = xs[:, 3 * w:3 * w + 2 * RWKV_RANK]
    z = xs[:, 3 * w + 2 * RWKV_RANK:]
    lane = lax.broadcasted_iota(jnp.int32, (length, 2 * RWKV_RANK), 1)
    low = jnp.where(lane < RWKV_RANK, jnp.tanh(low), low)
    up = _dot(_bf(low), wa_up_ref[...])
    yield
    w_logit = -_softplus(-(w0_ref[...] + up[:, 0:w])) - 0.5
    log_w = -jnp.exp(w_logit)
    a = _sigmoid(a0_ref[...] + up[:, w:2 * w])
    kk_raw = k * kk_ref[...]
    k2 = k * (1.0 + (a - 1.0) * ka_ref[...])
    rk_bonus = r * k2 * rk_ref[...]
    tri_b = _bf(jnp.where(_tri(length), 1.0, 0.0))
    cum = _cumsum_time(log_w, tri_b)
    yield
    p_in = jnp.exp(cum)
    p_ex = jnp.exp(cum - log_w)
    p_inv = jnp.exp(-cum)
    cum_end = cum[length - 1:length]
    p_end = jnp.exp(cum_end - cum)
    dec_end = jnp.exp(cum_end)

    even = lax.broadcasted_iota(jnp.int32, (length, LANES), 1) < dh
    rows2 = lax.broadcasted_iota(jnp.int32, (l2, l2), 0)
    cols2 = lax.broadcasted_iota(jnp.int32, (l2, l2), 1)
    same = (rows2 >= length) == (cols2 >= length)
    strict_bd = same & (rows2 > cols2)
    incl_bd = same & (rows2 >= cols2)
    rows_s = lax.broadcasted_iota(jnp.int32, (LANES, LANES), 0)
    cols_s = lax.broadcasted_iota(jnp.int32, (LANES, LANES), 1)
    state_bd = (rows_s >= dh) == (cols_s >= dh)
    n_double = int(math.log2(length))
    assert 2 ** n_double == length

    def stack(x):
        return jnp.concatenate([jnp.where(even, x, 0.0), jnp.where(even, 0.0, x)], axis=0)

    kas, lhss, grams = [], [], []
    for p in pairs:
        sl = sls[p]
        kk_r = kk_raw[:, sl]
        kk = kk_r / jnp.maximum(jnp.sqrt(_pair_sum(kk_r * kk_r, even)), 1e-12)
        ka = kk * a[:, sl]
        lhs = _bf(jnp.concatenate([stack(-kk * p_ex[:, sl]), stack(r[:, sl] * p_in[:, sl])], axis=0))
        bh = _bf(ka * p_inv[:, sl])
        kh = _bf(k2[:, sl] * p_inv[:, sl])
        rhs = jnp.concatenate([bh, bh, kh, kh], axis=0)
        grams.append(_dot_nt(lhs, rhs))
        kas.append(ka)
        lhss.append(lhs)
    yield
    uy0s = [_dot_nt(lhss[p], _bf(bds[p])) for p in pairs]
    v_stacks = [_bf(stack(v[:, sl])) for sl in sls]
    yield
    n_mats = [jnp.where(strict_bd, grams[p][0:l2, 0:l2], 0.0) for p in pairs]
    xs_u = [uy0s[p][0:l2] + _dot(_bf(jnp.where(strict_bd, grams[p][0:l2, l2:], 0.0)), v_stacks[p])
            for p in pairs]
    abks = [_bf(jnp.concatenate([jnp.where(incl_bd, grams[p][l2:, 0:l2], 0.0),
                                 jnp.where(incl_bd, grams[p][l2:, l2:], 0.0)], axis=1))
            for p in pairs]
    yield
    for step in range(n_double):
        for p in pairs:
            n_b = _bf(n_mats[p])
            xs_u[p] = xs_u[p] + _dot(n_b, _bf(xs_u[p]))
            if step + 1 < n_double:
                n_mats[p] = _dot(n_b, n_b)
        yield

    new_bd, ys = [], []
    for p in pairs:
        sl = sls[p]
        u_st = xs_u[p]
        y_st = uy0s[p][l2:] + _dot(abks[p], jnp.concatenate([_bf(u_st), v_stacks[p]], axis=0))
        ys.append(y_st[0:length] + y_st[length:])
        u_pair = u_st[0:length] + u_st[length:]
        upd = _dot_tn(_bf(jnp.concatenate([u_pair, v[:, sl]], axis=0)),
                      _bf(jnp.concatenate([kas[p] * p_end[:, sl], k2[:, sl] * p_end[:, sl]], axis=0)))
        new_bd.append(bds[p] * dec_end[:, sl] + jnp.where(state_bd, upd, 0.0))
    yield
    outs = []
    for p in pairs:
        sl = sls[p]
        mu = _pair_sum(ys[p], even) * (1.0 / dh)
        cen = ys[p] - mu
        var = _pair_sum(cen * cen, even) * (1.0 / dh)
        y = cen * lax.rsqrt(var + RWKV_GN_EPS) * gn_g[:, sl] + gn_b[:, sl]
        bonus = _pair_sum(rk_bonus[:, sl], even) * v[:, sl]
        outs.append(_bf((y + bonus) * _silu(z[:, sl])))

    def store():
        y_ref[...] = jnp.concatenate(outs, axis=1)
        carry_scr[...] = pc[length - 1:length, :]
        for p in pairs:
            bd_scr[p] = new_bd[p]

    stores.append(store)


def _branch_kernel(pa_ref, pb_ref, pc_ref, bias_i_ref, bias_f_ref, anorm_ref, lb_ref, hnorm_ref,
                   mu_ref, wa_up_ref, w0_ref, a0_ref, kk_ref, ka_ref, rk_ref, gg_ref, gb_ref,
                   c0_ref, n0_ref, m0_ref, hs0_ref, rs0_ref, shift0_ref,
                   ya_ref, yb_ref, yc_ref, c_out, n_out, m_out, hs_out, rs_out, shift_out,
                   cn_scr, m_scr, st_scr, bd_scr, carry_scr, *, length):
    c_idx = pl.program_id(1)
    dh = MLSTM_DH
    rh = RWKV_DH

    @pl.when(c_idx == 0)
    def _():
        n_t = n0_ref[...].T
        for h in range(MLSTM_H):
            cn_scr[h, :, 0:dh] = c0_ref[h].T
            cn_scr[h, :, dh:2 * dh] = jnp.broadcast_to(n_t[:, h:h + 1], (dh, dh))
        m_scr[...] = m0_ref[...]
        for h in range(HGRN_H):
            st_scr[h] = hs0_ref[h].T
        zero = jnp.zeros((rh, rh), F32)
        for p in range(RWKV_H // 2):
            top = jnp.concatenate([rs0_ref[2 * p], zero], axis=1)
            bot = jnp.concatenate([zero, rs0_ref[2 * p + 1]], axis=1)
            bd_scr[p] = jnp.concatenate([top, bot], axis=0)
        carry_scr[...] = shift0_ref[...]

    stores = []
    active = [
        _rwkv_stages(pc_ref, mu_ref, wa_up_ref, w0_ref, a0_ref, kk_ref, ka_ref, rk_ref, gg_ref,
                     gb_ref, yc_ref, bd_scr, carry_scr, stores, length=length),
        _mlstm_stages(pa_ref, pc_ref, bias_i_ref, bias_f_ref, anorm_ref, ya_ref, cn_scr, m_scr,
                      stores, length=length),
        _hgrn_stages(pb_ref, lb_ref, hnorm_ref, yb_ref, st_scr, stores, length=length),
    ]
    while active:
        for gen in list(active):
            if next(gen, StopIteration) is StopIteration:
                active.remove(gen)
    for store in stores:
        store()

    @pl.when(c_idx == pl.num_programs(1) - 1)
    def _():
        for h in range(MLSTM_H):
            c_out[h] = cn_scr[h, :, 0:dh].T
            n_out[h:h + 1, :] = cn_scr[h, :, dh:2 * dh].T[0:1, :]
        m_out[...] = m_scr[...]
        for h in range(HGRN_H):
            hs_out[h] = st_scr[h].T
        for p in range(RWKV_H // 2):
            rs_out[2 * p] = bd_scr[p, 0:rh, 0:rh]
            rs_out[2 * p + 1] = bd_scr[p, rh:2 * rh, rh:2 * rh]
        shift_out[...] = carry_scr[...]


def _branch_call(pa, pb, pc, p, states, layer, state_layer, *, n_seq, n_chunks, length, row_off,
                 name):
    assert row_off % length == 0
    blk0 = row_off // length
    rows = n_seq * n_chunks * length
    in_rows = lambda b, c: (blk0 + b * n_chunks + c, 0)
    out_rows = lambda b, c: (b * n_chunks + c, 0)
    state_tails = [(MLSTM_H, MLSTM_DH, MLSTM_DH), (MLSTM_H, MLSTM_DH), (1, LANES),
                   (HGRN_H, HGRN_DK, HGRN_DV), (RWKV_H, RWKV_DH, RWKV_DH), (1, C_COLS)]
    state_in = [pl.BlockSpec((None, None) + t, lambda b, c, t=t: (state_layer, b) + (0,) * len(t))
                for t in state_tails]
    state_out = [pl.BlockSpec((None,) + t, lambda b, c, t=t: (b,) + (0,) * len(t))
                 for t in state_tails]
    state_shapes = [jax.ShapeDtypeStruct((n_seq,) + t, F32) for t in state_tails]
    params = [p['bias_i'], p['bias_f'], p['mlstm_norm'], p['lb'], p['hgrn_norm'], p['mu'],
              p['wa_up'], p['w0'], p['a0'], p['k_k'], p['k_a'], p['r_k'], p['gn_g'], p['gn_b']]
    outs = pl.pallas_call(
        functools.partial(_branch_kernel, length=length),
        grid=(n_seq, n_chunks),
        in_specs=[pl.BlockSpec((length, A_W), in_rows), pl.BlockSpec((length, B_W), in_rows),
                  pl.BlockSpec((length, C_W), in_rows)]
        + [_layer_spec(a, layer) for a in params] + state_in,
        out_specs=[pl.BlockSpec((length, MLSTM_W), out_rows),
                   pl.BlockSpec((length, HGRN_W), out_rows),
                   pl.BlockSpec((length, RWKV_W), out_rows)] + state_out,
        out_shape=[jax.ShapeDtypeStruct((rows, MLSTM_W), BF16),
                   jax.ShapeDtypeStruct((rows, HGRN_W), BF16),
                   jax.ShapeDtypeStruct((rows, RWKV_W), BF16)] + state_shapes,
        scratch_shapes=[pltpu.VMEM((MLSTM_H, MLSTM_DH, 2 * MLSTM_DH), F32),
                        pltpu.VMEM((1, LANES), F32),
                        pltpu.VMEM((HGRN_H, HGRN_DV, HGRN_DK), F32),
                        pltpu.VMEM((RWKV_H // 2, 2 * RWKV_DH, 2 * RWKV_DH), F32),
                        pltpu.VMEM((1, C_COLS), F32)],
        compiler_params=pltpu.CompilerParams(
            dimension_semantics=("parallel", "arbitrary"), vmem_limit_bytes=VMEM_LIMIT),
        name=name,
    )(pa, pb, pc, *params, *states)
    return outs[0:3], outs[3:]


def _layer_weights(w):
    a_cols = 5 * MLSTM_W + 2 * MLSTM_H
    if0 = 3 * MLSTM_W
    b0 = a_cols
    c0 = b0 + B_W
    g0 = c0 + C_COLS
    w_a = _bf(jnp.concatenate([w[:, 0:if0], w[:, if0 + 2 * MLSTM_H:a_cols]], axis=1))
    w_b = _bf(w[:, b0:c0])
    gate_pad = jnp.zeros((D_MODEL, LANES - MLSTM_H), F32)
    w_c = _bf(jnp.concatenate(
        [w[:, c0:g0], w[:, if0:if0 + MLSTM_H], gate_pad,
         w[:, if0 + MLSTM_H:if0 + 2 * MLSTM_H], gate_pad,
         jnp.zeros((D_MODEL, LANES), F32)], axis=1))
    n_chunk = D_MODEL // MERGE_NC
    w_g = _bf(jnp.concatenate(
        [w[:, g0 + b * D_MODEL + j * MERGE_NC:g0 + b * D_MODEL + (j + 1) * MERGE_NC]
         for j in range(n_chunk) for b in range(3)], axis=1))
    return w_a, w_b, w_c, w_g


def _lane_pad(m):
    pad = [(0, 0)] * (m.ndim - 1) + [(0, LANES - m.shape[-1])]
    return jnp.expand_dims(jnp.pad(m.astype(F32), pad), -2)


def _stacked_params(lb_all, norm_pre, norm_post, mlstm_b_i, mlstm_b_f, mlstm_norm, hgrn_norm,
                    rwkv_mu, rwkv_w0, rwkv_w_up, rwkv_a0, rwkv_a_up, rwkv_k_k, rwkv_k_a, rwkv_r_k,
                    rwkv_gn_g, rwkv_gn_b, w_proj_a, w_proj_b, w_proj_c, w_out):
    row = lambda a: a.astype(F32)[:, None, :]
    zero = jnp.zeros(rwkv_w_up.shape, F32)
    wa_up = _bf(jnp.concatenate([jnp.concatenate([rwkv_w_up, zero], axis=2),
                                 jnp.concatenate([zero, rwkv_a_up], axis=2)], axis=1))
    return dict(
        norm_pre=row(norm_pre), norm_post=row(norm_post), bias_i=_lane_pad(mlstm_b_i),
        bias_f=_lane_pad(mlstm_b_f), mlstm_norm=row(mlstm_norm), lb=row(lb_all),
        hgrn_norm=row(hgrn_norm), mu=row(rwkv_mu), wa_up=wa_up, w0=row(rwkv_w0), a0=row(rwkv_a0),
        k_k=row(rwkv_k_k), k_a=row(rwkv_k_a), r_k=row(rwkv_r_k), gn_g=row(rwkv_gn_g),
        gn_b=row(rwkv_gn_b), wpa=_bf(w_proj_a), wpb=_bf(w_proj_b), wpc=_bf(w_proj_c),
        wo=_bf(w_out))


def _branches(p, layer, proj_main, proj_tail, st_s, *, bp, t_p, bs, t_s):
    tails = [(MLSTM_H, MLSTM_DH, MLSTM_DH), (MLSTM_H, MLSTM_DH), (1, LANES),
             (HGRN_H, HGRN_DK, HGRN_DV), (RWKV_H, RWKV_DH, RWKV_DH), (1, C_COLS)]
    st_zero = tuple(jnp.zeros((1, bp) + t, F32) for t in tails)
    y_meta, st_meta = _branch_call(*proj_tail, p, st_zero, layer, 0, n_seq=bp, n_chunks=1,
                                   length=N_META, row_off=0, name=f'branch_meta{layer}')
    y_main, st_p = _branch_call(*proj_main, p, tuple(s[None] for s in st_meta), layer, 0,
                                n_seq=bp, n_chunks=t_p // CHUNK, length=CHUNK, row_off=0,
                                name=f'branch_main{layer}')
    y_samp, st_so = _branch_call(*proj_tail, p, st_s, layer, layer, n_seq=bs, n_chunks=1,
                                 length=t_s, row_off=bp * N_META, name=f'branch_samp{layer}')
    y_tail = tuple(jnp.concatenate([a, b], axis=0) for a, b in zip(y_meta, y_samp))
    unpad = lambda st: (st[0], st[1], st[2][:, 0, 0:MLSTM_H]) + tuple(st[3:])
    return y_main, y_tail, unpad(st_p), unpad(st_so)


def kernel(x_prompt, x_sample, state_mlstm_C, state_mlstm_n, state_mlstm_m, state_hgrn_S,
           state_rwkv_S, cache_rwkv_shift, meta_tokens, norm_pre, norm_post, w_in,
           mlstm_b_i, mlstm_b_f, mlstm_norm, hgrn_lb_logits, hgrn_norm, rwkv_mu, rwkv_w0,
           rwkv_w_up, rwkv_a0, rwkv_a_up, rwkv_k_k, rwkv_k_a, rwkv_r_k, rwkv_gn_g, rwkv_gn_b,
           w_proj_a, w_proj_b, w_proj_c, w_out):
    bp, t_p, _ = x_prompt.shape
    bs, t_s, _ = x_sample.shape
    depth = w_in.shape[0]
    assert t_p % CHUNK == 0 and t_s % HGRN_SUB == 0 and (bp * N_META) % t_s == 0

    sm = jax.nn.softmax(hgrn_lb_logits.astype(F32), axis=0)
    lb_all = jnp.cumsum(sm, axis=0) - sm[0]

    x_main = x_prompt.reshape(bp * t_p, D_MODEL)
    meta = jnp.broadcast_to(meta_tokens.astype(F32)[None], (bp, N_META, D_MODEL))
    x_tail = jnp.concatenate([meta.reshape(bp * N_META, D_MODEL),
                              x_sample.reshape(bs * t_s, D_MODEL)], axis=0)

    p = _stacked_params(lb_all, norm_pre, norm_post, mlstm_b_i, mlstm_b_f, mlstm_norm, hgrn_norm,
                        rwkv_mu, rwkv_w0, rwkv_w_up, rwkv_a0, rwkv_a_up, rwkv_k_k, rwkv_k_a,
                        rwkv_r_k, rwkv_gn_g, rwkv_gn_b, w_proj_a, w_proj_b, w_proj_c, w_out)
    st_s = (state_mlstm_C.astype(F32), state_mlstm_n.astype(F32), _lane_pad(state_mlstm_m),
            state_hgrn_S.astype(F32), state_rwkv_S.astype(F32), cache_rwkv_shift.astype(F32))

    outs_p, outs_s = [], []
    for l in range(depth):
        weights = _layer_weights(w_in[l])
        names = ('w_a', 'w_b', 'w_c')
        proj_main = tuple(_proj(x_main, p['norm_pre'], l, w, f'proj_{k}_main{l}')
                          for k, w in zip(names, weights[0:3]))
        proj_tail = tuple(_proj(x_tail, p['norm_pre'], l, w, f'proj_{k}_tail{l}')
                          for k, w in zip(names, weights[0:3]))
        y_main, y_tail, st_p_out, st_s_out = _branches(
            p, l, proj_main, proj_tail, st_s, bp=bp, t_p=t_p, bs=bs, t_s=t_s)
        merge_args = (weights[3], p['wpa'], p['wpb'], p['wpc'], p['wo'], p['norm_post'], l)
        x_main = _merge(x_main, p['norm_pre'], *y_main, *merge_args, f'merge_main{l}')
        x_tail = _merge(x_tail, p['norm_pre'], *y_tail, *merge_args, f'merge_tail{l}')
        outs_p.append(st_p_out)
        outs_s.append(st_s_out)

    states_p = tuple(jnp.stack([o[j] for o in outs_p]) for j in range(6))
    states_s = tuple(jnp.stack([o[j] for o in outs_s]) for j in range(6))
    y_prompt = x_main.reshape(bp, t_p, D_MODEL)
    y_sample = x_tail[bp * N_META:].reshape(bs, t_s, D_MODEL)
    return (y_prompt, y_sample) + states_p + states_s
```

```python
import functools
import math

import jax
import jax.numpy as jnp
from jax import lax
from jax.experimental import pallas as pl
from jax.experimental.pallas import tpu as pltpu

F32 = jnp.float32
BF16 = jnp.bfloat16

D_MODEL = 2048
CHUNK = 64
N_META = 16
EPS = 1e-6

MLSTM_H = 8
MLSTM_DH = 128
MLSTM_W = MLSTM_H * MLSTM_DH
HGRN_H = 4
HGRN_DK = 128
HGRN_DV = 128
HGRN_W = HGRN_H * HGRN_DV
RWKV_H = 8
RWKV_DH = 64
RWKV_W = RWKV_H * RWKV_DH
RWKV_RANK = 64
RWKV_GN_EPS = 64e-5
C_COLS = 4 * RWKV_W + 2 * RWKV_RANK

LANES = 128
HGRN_SUB = 8
LOG2E = 1.4426950408889634
A_W = 5 * MLSTM_W
B_W = 4 * HGRN_W
C_W = C_COLS + 3 * LANES
IF_BLOCK = C_COLS // LANES
MERGE_NC = 512
BRANCH_GROUP = 2
VMEM_LIMIT = 56 * 1024 * 1024


def _dot(a, b):
    return jnp.dot(a, b, preferred_element_type=F32)


def _dot_nt(a, b):
    return lax.dot_general(a, b, (((1,), (1,)), ((), ())), preferred_element_type=F32)


def _dot_tn(a, b):
    return lax.dot_general(a, b, (((0,), (0,)), ((), ())), preferred_element_type=F32)


def _bf(a):
    return a.astype(BF16)


def _sigmoid(x):
    return 1.0 / (1.0 + jnp.exp(-x))


def _silu(x):
    return x * _sigmoid(x)


def _softplus(x):
    return jnp.maximum(x, 0.0) + jnp.log1p(jnp.exp(-jnp.abs(x)))


def _tri(length, strict=False):
    row = lax.broadcasted_iota(jnp.int32, (length, length), 0)
    col = lax.broadcasted_iota(jnp.int32, (length, length), 1)
    return (row > col) if strict else (row >= col)


def _cumsum_time(x, tri_b):
    hi = _bf(x)
    r1 = x - hi.astype(F32)
    mid = _bf(r1)
    lo = _bf(r1 - mid.astype(F32))
    return _dot(tri_b, hi) + _dot(tri_b, mid) + _dot(tri_b, lo)


def _rmsnorm(x, g):
    return x * lax.rsqrt(jnp.mean(x * x, axis=-1, keepdims=True) + EPS) * g


def _row_tile(rows, cap):
    best = None
    for t in range(16, min(rows, cap) + 1, 16):
        if rows % t == 0:
            best = t
    assert best is not None, rows
    return best


def _col_tile(cols, cap):
    best = None
    for t in range(LANES, min(cols, cap) + 1, LANES):
        if cols % t == 0:
            best = t
    assert best is not None, cols
    return best


def _proj_kernel(x_ref, g_ref, w_ref, o_ref):
    h = _bf(_rmsnorm(x_ref[...], g_ref[...]))
    o_ref[...] = _dot(h, w_ref[...])


def _layer_spec(arr, layer):
    tail = arr.shape[1:]
    return pl.BlockSpec((None,) + tail, lambda *_: (layer,) + (0,) * len(tail))


def _proj(x, g, layer, w, name):
    rows, _ = x.shape
    cols = w.shape[1]
    tm = _row_tile(rows, 1024)
    tn = _col_tile(cols, 1280)
    return pl.pallas_call(
        _proj_kernel,
        grid=(cols // tn, rows // tm),
        in_specs=[
            pl.BlockSpec((tm, D_MODEL), lambda j, i: (i, 0)),
            _layer_spec(g, layer),
            pl.BlockSpec((D_MODEL, tn), lambda j, i: (0, j)),
        ],
        out_specs=pl.BlockSpec((tm, tn), lambda j, i: (i, j)),
        out_shape=jax.ShapeDtypeStruct((rows, cols), F32),
        compiler_params=pltpu.CompilerParams(
            dimension_semantics=("parallel", "parallel"), vmem_limit_bytes=VMEM_LIMIT),
        name=name,
    )(x, g, w)


def _merge_kernel(x_ref, gpre_ref, ya_ref, yb_ref, yc_ref, wg_ref,
                  wpa_ref, wpb_ref, wpc_ref, wo_ref, gpost_ref, o_ref, h_scr, acc_scr):
    j = pl.program_id(1)
    nc = MERGE_NC

    @pl.when(j == 0)
    def _():
        h_scr[...] = _bf(_rmsnorm(x_ref[...], gpre_ref[...]))
        acc_scr[...] = jnp.zeros_like(acc_scr)

    gates = _sigmoid(_dot(h_scr[...], wg_ref[...]))
    merged = (gates[:, 0:nc] * _dot(ya_ref[...], wpa_ref[...])
              + gates[:, nc:2 * nc] * _dot(yb_ref[...], wpb_ref[...])
              + gates[:, 2 * nc:3 * nc] * _dot(yc_ref[...], wpc_ref[...]))
    acc_scr[...] += _dot(_bf(merged), wo_ref[...])

    @pl.when(j == pl.num_programs(1) - 1)
    def _():
        o_ref[...] = x_ref[...] + _rmsnorm(acc_scr[...], gpost_ref[...])


def _merge(x, gpre, ya, yb, yc, wg, wpa, wpb, wpc, wo, gpost, layer, name):
    rows = x.shape[0]
    tm = _row_tile(rows, 640)
    nc = MERGE_NC
    n_chunk = D_MODEL // nc
    row = lambda i, j: (i, 0)
    return pl.pallas_call(
        _merge_kernel,
        grid=(rows // tm, n_chunk),
        in_specs=[
            pl.BlockSpec((tm, D_MODEL), row),
            _layer_spec(gpre, layer),
            pl.BlockSpec((tm, MLSTM_W), row),
            pl.BlockSpec((tm, HGRN_W), row),
            pl.BlockSpec((tm, RWKV_W), row),
            pl.BlockSpec((D_MODEL, 3 * nc), lambda i, j: (0, j)),
            pl.BlockSpec((None, MLSTM_W, nc), lambda i, j: (layer, 0, j)),
            pl.BlockSpec((None, HGRN_W, nc), lambda i, j: (layer, 0, j)),
            pl.BlockSpec((None, RWKV_W, nc), lambda i, j: (layer, 0, j)),
            pl.BlockSpec((None, nc, D_MODEL), lambda i, j: (layer, j, 0)),
            _layer_spec(gpost, layer),
        ],
        out_specs=pl.BlockSpec((tm, D_MODEL), row),
        out_shape=jax.ShapeDtypeStruct((rows, D_MODEL), F32),
        scratch_shapes=[pltpu.VMEM((tm, D_MODEL), BF16), pltpu.VMEM((tm, D_MODEL), F32)],
        compiler_params=pltpu.CompilerParams(
            dimension_semantics=("parallel", "arbitrary"), vmem_limit_bytes=VMEM_LIMIT),
        name=name,
    )(x, gpre, ya, yb, yc, wg, wpa, wpb, wpc, wo, gpost)


def _lane_mean(x, j_b):
    n = x.shape[0]
    hi = _bf(x)
    mid = _bf(x - hi.astype(F32))
    out = _dot(jnp.concatenate([hi, mid], axis=0), j_b)
    return out[0:n] + out[n:]


def _mlstm_stages(pa_ref, pc_ref, bias_i_ref, bias_f_ref, norm_ref, y_ref, cn_scr, m_scr, stores,
                  *, length):
    dh = MLSTM_DH
    heads = range(MLSTM_H)
    sls = [slice(h * dh, (h + 1) * dh) for h in heads]
    col = lambda part, h: slice(part * MLSTM_W + h * dh, part * MLSTM_W + (h + 1) * dh)
    cns = [cn_scr[h] for h in heads]
    m_prev = m_scr[...]

    causal = _tri(length)
    tri_b = _bf(jnp.where(causal, 1.0, 0.0))
    gate0 = IF_BLOCK * LANES
    ig = pc_ref[:, gate0:gate0 + LANES] + bias_i_ref[...]
    fg = pc_ref[:, gate0 + LANES:gate0 + 2 * LANES] + bias_f_ref[...]
    log_f = jnp.minimum(fg, 0.0) - jnp.log1p(jnp.exp(-jnp.abs(fg)))
    b_all = _cumsum_time(log_f, tri_b)
    yield
    c_all = ig - b_all
    row = lax.broadcasted_iota(jnp.int32, (length, LANES), 0)
    run_max = c_all
    shift = 1
    while shift < length:
        run_max = jnp.maximum(
            run_max, jnp.where(row >= shift, pltpu.roll(run_max, shift, 0), -jnp.inf))
        shift *= 2
    mx_all = jnp.maximum(run_max, m_prev)
    mt_all = b_all + mx_all
    mx_end = mx_all[length - 1:length]
    m_new = b_all[length - 1:length] + mx_end
    w_old_all = jnp.exp(m_prev - mx_end)
    c_t = c_all.T
    ones = jnp.ones((length, dh), F32)
    j_b = jnp.full((dh, dh), 1.0 / dh, BF16)
    scale = MLSTM_DH ** -0.5
    yield
    k_ts = [(pa_ref[:, col(1, h)] * scale).T for h in heads]
    qbs = [_bf(pa_ref[:, col(0, h)]) for h in heads]
    s_raw = [_dot(qbs[h], _bf(k_ts[h])) for h in heads]
    yield
    inter = [_dot(qbs[h], _bf(cns[h])) for h in heads]
    yield
    mx_bs = [jnp.broadcast_to(mx_all[:, h:h + 1], (length, dh)) for h in heads]
    es = [jnp.exp(jnp.where(causal, c_t[h:h + 1, :] - mx_bs[h][:, 0:length], -jnp.inf))
          for h in heads]
    yield
    v1s = [_bf(jnp.concatenate([pa_ref[:, col(2, h)], ones], axis=1)) for h in heads]
    tots = []
    for h in heads:
        w_inter = jnp.exp(m_prev[:, h:h + 1] - mx_bs[h])
        tots.append(jnp.concatenate([w_inter, w_inter], axis=1) * inter[h]
                    + _dot(_bf(s_raw[h] * es[h]), v1s[h]))
    yield
    new_cn = [w_old_all[:, h:h + 1] * cns[h]
              + _dot(_bf(k_ts[h] * es[h][length - 1:length, :]), v1s[h]) for h in heads]
    yield
    cens = [tots[h][:, 0:dh] - _lane_mean(tots[h][:, 0:dh], j_b) for h in heads]
    yield
    vars_ = [_lane_mean(cens[h] * cens[h], j_b) for h in heads]
    yield
    outs = []
    for h in heads:
        mt_b = jnp.broadcast_to(mt_all[:, h:h + 1], (length, dh))
        inv = 1.0 / jnp.maximum(jnp.abs(tots[h][:, dh:]), jnp.exp(-mt_b))
        hid = cens[h] * inv * lax.rsqrt(vars_[h] * inv * inv + EPS) * norm_ref[:, sls[h]]
        outs.append(_bf(hid * _sigmoid(pa_ref[:, col(3, h)]) * _silu(pa_ref[:, col(4, h)])))

    def store():
        y_ref[...] = jnp.concatenate(outs, axis=1)
        for h in heads:
            cn_scr[h] = new_cn[h]
        m_scr[...] = m_new

    stores.append(store)


def _hgrn_stages(pb_ref, lb_ref, norm_ref, y_ref, st_scr, stores, *, length):
    dk = HGRN_DK
    sub = min(HGRN_SUB, length)
    n_sub = length // sub
    heads = range(HGRN_H)
    sls = [slice(h * dk, (h + 1) * dk) for h in heads]
    col = lambda part, h: slice(part * HGRN_W + h * dk, part * HGRN_W + (h + 1) * dk)
    sts = [st_scr[h] for h in heads]

    lb = lb_ref[...]
    fp = pb_ref[:, HGRN_W:2 * HGRN_W]
    f_gate = lb + (1.0 - lb) * _sigmoid(fp)
    k_all = (1.0 - lb) * _sigmoid(-fp)
    tri_b = _bf(jnp.where(_tri(length), 1.0, 0.0))
    a_all = _cumsum_time(jnp.log(f_gate), tri_b)
    lane_l = lax.broadcasted_iota(jnp.int32, (sub, length), 1)
    row_l = lax.broadcasted_iota(jnp.int32, (sub, length), 0)
    yield
    qs = [pb_ref[:, col(0, h)] for h in heads]
    ks = [k_all[:, sl] for sl in sls]
    avs = [a_all[:, sl] for sl in sls]
    a2_all = a_all * LOG2E
    a2s = [a2_all[:, sl] for sl in sls]
    vbs = [_bf(pb_ref[:, col(2, h)]) for h in heads]
    o_inter = [_dot_nt(_bf(qs[h] * jnp.exp(avs[h])), _bf(sts[h])) for h in heads]
    yield
    new_st = []
    for h in heads:
        a_end = avs[h][length - 1:length]
        new_st.append(jnp.exp(a_end) * sts[h]
                      + _dot_tn(vbs[h], _bf(ks[h] * jnp.exp(a_end - avs[h]))))
    yield

    blocks = [[] for _ in heads]
    for i in range(n_sub):
        r0 = i * sub
        blks = []
        for h in heads:
            if i > 0:
                ref_row = avs[h][r0:r0 + 1]
                q_s = _bf(qs[h][r0:r0 + sub] * jnp.exp(avs[h][r0:r0 + sub] - ref_row))
                k_s = jnp.concatenate(
                    [ks[h][0:r0] * jnp.exp(ref_row - avs[h][0:r0]),
                     jnp.zeros((length - r0, dk), F32)], axis=0)
                blks.append(_dot_nt(q_s, _bf(k_s)))
            else:
                blks.append(jnp.zeros((sub, length), F32))
        for s_idx in range(sub):
            for h in heads:
                a_i = a2s[h][r0:r0 + sub]
                e = jnp.exp2(jnp.minimum(a_i - a_i[s_idx:s_idx + 1], 0.0))
                col_v = jnp.sum(qs[h][r0:r0 + sub] * e * ks[h][r0 + s_idx:r0 + s_idx + 1],
                                axis=-1, keepdims=True)
                blks[h] = jnp.where(lane_l == r0 + s_idx, col_v, blks[h])
        yield
        for h in heads:
            blocks[h].append(jnp.where(lane_l <= r0 + row_l, blks[h], 0.0))

    outs = []
    for h in heads:
        scores = jnp.concatenate(blocks[h], axis=0) if n_sub > 1 else blocks[h][0]
        o = o_inter[h] + _dot(_bf(scores), vbs[h])
        o = o * lax.rsqrt(jnp.mean(o * o, axis=-1, keepdims=True) + EPS)
        outs.append(_bf(o * norm_ref[:, sls[h]] * _silu(pb_ref[:, col(3, h)])))

    def store():
        y_ref[...] = jnp.concatenate(outs, axis=1)
        for h in heads:
            st_scr[h] = new_st[h]

    stores.append(store)


def _pair_sum(x, even):
    s_even = jnp.sum(jnp.where(even, x, 0.0), axis=-1, keepdims=True)
    s_odd = jnp.sum(jnp.where(even, 0.0, x), axis=-1, keepdims=True)
    return jnp.where(even, s_even, s_odd)


def _rwkv_stages(pc_ref, mu_ref, wa_up_ref, w0_ref, a0_ref, kk_ref, ka_ref, rk_ref, gg_ref, gb_ref,
                 y_ref, bd_scr, carry_scr, stores, *, length):
    dh = RWKV_DH
    w = RWKV_W
    pairs = range(RWKV_H // 2)
    sls = [slice(p * LANES, (p + 1) * LANES) for p in pairs]
    l2 = 2 * length
    bds = [bd_scr[p] for p in pairs]
    gn_g = gg_ref[...]
    gn_b = gb_ref[...]

    pc = pc_ref[:, 0:C_COLS]
    row0 = lax.broadcasted_iota(jnp.int32, (length, C_COLS), 0) == 0
    prev = jnp.where(row0, carry_scr[...], pltpu.roll(pc, 1, 0))
    xs = pc + mu_ref[...] * (prev - pc)
    r = xs[:, 0:w]
    k = xs[:, w:2 * w]
    v = xs[:, 2 * w:3 * w]
    low = xs[:, 3 * w:3 * w + 2 * RWKV_RANK]
    z = xs[:, 3 * w + 2 * RWKV_RANK:]
    lane = lax.broadcasted_iota(jnp.int32, (length, 2 * RWKV_RANK), 1)
    low = jnp.where(lane < RWKV_RANK, jnp.tanh(low), low)
    up = _dot(_bf(low), wa_up_ref[...])
    yield
    w_logit = -_softplus(-(w0_ref[...] + up[:, 0:w])) - 0.5
    log_w = -jnp.exp(w_logit)
    a = _sigmoid(a0_ref[...] + up[:, w:2 * w])
    kk_raw = k * kk_ref[...]
    k2 = k * (1.0 + (a - 1.0) * ka_ref[...])
    rk_bonus = r * k2 * rk_ref[...]
    tri_b = _bf(jnp.where(_tri(length), 1.0, 0.0))
    cum = _cumsum_time(log_w, tri_b)
    yield
    p_in = jnp.exp(cum)
    p_ex = jnp.exp(cum - log_w)
    p_inv = jnp.exp(-cum)
    cum_end = cum[length - 1:length]
    p_end = jnp.exp(cum_end - cum)
    dec_end = jnp.exp(cum_end)

    even = lax.broadcasted_iota(jnp.int32, (length, LANES), 1) < dh
    rows2 = lax.broadcasted_iota(jnp.int32, (l2, l2), 0)
    cols2 = lax.broadcasted_iota(jnp.int32, (l2, l2), 1)
    same = (rows2 >= length) == (cols2 >= length)
    strict_bd = same & (rows2 > cols2)
    incl_bd = same & (rows2 >= cols2)
    rows_s = lax.broadcasted_iota(jnp.int32, (LANES, LANES), 0)
    cols_s = lax.broadcasted_iota(jnp.int32, (LANES, LANES), 1)
    state_bd = (rows_s >= dh) == (cols_s >= dh)
    n_double = int(math.log2(length))
    assert 2 ** n_double == length

    def stack(x):
        return jnp.concatenate([jnp.where(even, x, 0.0), jnp.where(even, 0.0, x)], axis=0)

    kas, lhss, grams = [], [], []
    for p in pairs:
        sl = sls[p]
        kk_r = kk_raw[:, sl]
        kk = kk_r / jnp.maximum(jnp.sqrt(_pair_sum(kk_r * kk_r, even)), 1e-12)
        ka = kk * a[:, sl]
        lhs = _bf(jnp.concatenate([stack(-kk * p_ex[:, sl]), stack(r[:, sl] * p_in[:, sl])], axis=0))
        bh = _bf(ka * p_inv[:, sl])
        kh = _bf(k2[:, sl] * p_inv[:, sl])
        rhs = jnp.concatenate([bh, bh, kh, kh], axis=0)
        grams.append(_dot_nt(lhs, rhs))
        kas.append(ka)
        lhss.append(lhs)
    yield
    uy0s = [_dot_nt(lhss[p], _bf(bds[p])) for p in pairs]
    v_stacks = [_bf(stack(v[:, sl])) for sl in sls]
    yield
    n_mats = [jnp.where(strict_bd, grams[p][0:l2, 0:l2], 0.0) for p in pairs]
    xs_u = [uy0s[p][0:l2] + _dot(_bf(jnp.where(strict_bd, grams[p][0:l2, l2:], 0.0)), v_stacks[p])
            for p in pairs]
    abks = [_bf(jnp.concatenate([jnp.where(incl_bd, grams[p][l2:, 0:l2], 0.0),
                                 jnp.where(incl_bd, grams[p][l2:, l2:], 0.0)], axis=1))
            for p in pairs]
    yield
    for step in range(n_double):
        for p in pairs:
            n_b = _bf(n_mats[p])
            xs_u[p] = xs_u[p] + _dot(n_b, _bf(xs_u[p]))
            if step + 1 < n_double:
                n_mats[p] = _dot(n_b, n_b)
        yield

    new_bd, ys = [], []
    for p in pairs:
        sl = sls[p]
        u_st = xs_u[p]
        y_st = uy0s[p][l2:] + _dot(abks[p], jnp.concatenate([_bf(u_st), v_stacks[p]], axis=0))
        ys.append(y_st[0:length] + y_st[length:])
        u_pair = u_st[0:length] + u_st[length:]
        upd = _dot_tn(_bf(jnp.concatenate([u_pair, v[:, sl]], axis=0)),
                      _bf(jnp.concatenate([kas[p] * p_end[:, sl], k2[:, sl] * p_end[:, sl]], axis=0)))
        new_bd.append(bds[p] * dec_end[:, sl] + jnp.where(state_bd, upd, 0.0))
    yield
    outs = []
    for p in pairs:
        sl = sls[p]
        mu = _pair_sum(ys[p], even) * (1.0 / dh)
        cen = ys[p] - mu
        var = _pair_sum(cen * cen, even) * (1.0 / dh)
        y = cen * lax.rsqrt(var + RWKV_GN_EPS) * gn_g[:, sl] + gn_b[:, sl]
        bonus = _pair_sum(rk_bonus[:, sl], even) * v[:, sl]
        outs.append(_bf((y + bonus) * _silu(z[:, sl])))

    def store():
        y_ref[...] = jnp.concatenate(outs, axis=1)
        carry_scr[...] = pc[length - 1:length, :]
        for p in pairs:
            bd_scr[p] = new_bd[p]

    stores.append(store)


def _branch_kernel(pa_ref, pb_ref, pc_ref, bias_i_ref, bias_f_ref, anorm_ref, lb_ref, hnorm_ref,
                   mu_ref, wa_up_ref, w0_ref, a0_ref, kk_ref, ka_ref, rk_ref, gg_ref, gb_ref,
                   c0_ref, n0_ref, m0_ref, hs0_ref, rs0_ref, shift0_ref,
                   ya_ref, yb_ref, yc_ref, c_out, n_out, m_out, hs_out, rs_out, shift_out,
                   cn_scr, m_scr, st_scr, bd_scr, carry_scr, *, length, n_grp):
    c_idx = pl.program_id(1)
    dh = MLSTM_DH
    rh = RWKV_DH
    streams = range(n_grp)

    def rows(ref, g):
        return ref.at[g] if len(ref.shape) == 3 else ref.at[pl.ds(g * length, length)]

    @pl.when(c_idx == 0)
    def _():
        for g in streams:
            n_t = n0_ref[g].T
            for h in range(MLSTM_H):
                cn_scr[g, h, :, 0:dh] = c0_ref[g, h].T
                cn_scr[g, h, :, dh:2 * dh] = jnp.broadcast_to(n_t[:, h:h + 1], (dh, dh))
            m_scr[g] = m0_ref[g]
            for h in range(HGRN_H):
                st_scr[g, h] = hs0_ref[g, h].T
            zero = jnp.zeros((rh, rh), F32)
            for p in range(RWKV_H // 2):
                top = jnp.concatenate([rs0_ref[g, 2 * p], zero], axis=1)
                bot = jnp.concatenate([zero, rs0_ref[g, 2 * p + 1]], axis=1)
                bd_scr[g, p] = jnp.concatenate([top, bot], axis=0)
            carry_scr[g] = shift0_ref[g]

    stores = []
    active = (
        [_rwkv_stages(rows(pc_ref, g), mu_ref, wa_up_ref, w0_ref, a0_ref, kk_ref, ka_ref, rk_ref,
                      gg_ref, gb_ref, yc_ref.at[g], bd_scr.at[g], carry_scr.at[g], stores,
                      length=length) for g in streams]
        + [_mlstm_stages(rows(pa_ref, g), rows(pc_ref, g), bias_i_ref, bias_f_ref, anorm_ref,
                         ya_ref.at[g], cn_scr.at[g], m_scr.at[g], stores, length=length)
           for g in streams]
        + [_hgrn_stages(rows(pb_ref, g), lb_ref, hnorm_ref, yb_ref.at[g], st_scr.at[g], stores,
                        length=length) for g in streams])
    while active:
        for gen in list(active):
            if next(gen, StopIteration) is StopIteration:
                active.remove(gen)
    for store in stores:
        store()

    @pl.when(c_idx == pl.num_programs(1) - 1)
    def _():
        for g in streams:
            for h in range(MLSTM_H):
                c_out[g, h] = cn_scr[g, h, :, 0:dh].T
                n_out[g, h:h + 1, :] = cn_scr[g, h, :, dh:2 * dh].T[0:1, :]
            m_out[g] = m_scr[g]
            for h in range(HGRN_H):
                hs_out[g, h] = st_scr[g, h].T
            for p in range(RWKV_H // 2):
                rs_out[g, 2 * p] = bd_scr[g, p, 0:rh, 0:rh]
                rs_out[g, 2 * p + 1] = bd_scr[g, p, rh:2 * rh, rh:2 * rh]
            shift_out[g] = carry_scr[g]


def _branch_call(pa, pb, pc, p, states, layer, state_layer, *, n_seq, n_chunks, length, row_off,
                 name):
    grp = BRANCH_GROUP
    assert n_seq % grp == 0
    if n_chunks == 1:
        assert row_off % (grp * length) == 0
        blk0 = row_off // (grp * length)
        views = (pa, pb, pc)
        row_spec = lambda width: pl.BlockSpec((grp * length, width), lambda b, c: (blk0 + b, 0))
    else:
        assert row_off == 0 and pa.shape[0] == n_seq * n_chunks * length
        views = tuple(a.reshape(n_seq, n_chunks * length, a.shape[1]) for a in (pa, pb, pc))
        row_spec = lambda width: pl.BlockSpec((grp, length, width), lambda b, c: (b, c, 0))
    out_spec = lambda width: pl.BlockSpec((grp, length, width), lambda b, c: (b, c, 0))
    state_tails = [(MLSTM_H, MLSTM_DH, MLSTM_DH), (MLSTM_H, MLSTM_DH), (1, LANES),
                   (HGRN_H, HGRN_DK, HGRN_DV), (RWKV_H, RWKV_DH, RWKV_DH), (1, C_COLS)]
    state_in = [pl.BlockSpec((None, grp) + t, lambda b, c, t=t: (state_layer, b) + (0,) * len(t))
                for t in state_tails]
    state_out = [pl.BlockSpec((grp,) + t, lambda b, c, t=t: (b,) + (0,) * len(t))
                 for t in state_tails]
    state_shapes = [jax.ShapeDtypeStruct((n_seq,) + t, F32) for t in state_tails]
    params = [p['bias_i'], p['bias_f'], p['mlstm_norm'], p['lb'], p['hgrn_norm'], p['mu'],
              p['wa_up'], p['w0'], p['a0'], p['k_k'], p['k_a'], p['r_k'], p['gn_g'], p['gn_b']]
    t_len = n_chunks * length
    outs = pl.pallas_call(
        functools.partial(_branch_kernel, length=length, n_grp=grp),
        grid=(n_seq // grp, n_chunks),
        in_specs=[row_spec(A_W), row_spec(B_W), row_spec(C_W)]
        + [_layer_spec(a, layer) for a in params] + state_in,
        out_specs=[out_spec(MLSTM_W), out_spec(HGRN_W), out_spec(RWKV_W)] + state_out,
        out_shape=[jax.ShapeDtypeStruct((n_seq, t_len, MLSTM_W), BF16),
                   jax.ShapeDtypeStruct((n_seq, t_len, HGRN_W), BF16),
                   jax.ShapeDtypeStruct((n_seq, t_len, RWKV_W), BF16)] + state_shapes,
        scratch_shapes=[pltpu.VMEM((grp, MLSTM_H, MLSTM_DH, 2 * MLSTM_DH), F32),
                        pltpu.VMEM((grp, 1, LANES), F32),
                        pltpu.VMEM((grp, HGRN_H, HGRN_DV, HGRN_DK), F32),
                        pltpu.VMEM((grp, RWKV_H // 2, 2 * RWKV_DH, 2 * RWKV_DH), F32),
                        pltpu.VMEM((grp, 1, C_COLS), F32)],
        compiler_params=pltpu.CompilerParams(
            dimension_semantics=("parallel", "arbitrary"), vmem_limit_bytes=VMEM_LIMIT),
        name=name,
    )(*views, *params, *states)
    ys = tuple(y.reshape(n_seq * t_len, y.shape[2]) for y in outs[0:3])
    return ys, outs[3:]


def _layer_weights(w):
    a_cols = 5 * MLSTM_W + 2 * MLSTM_H
    if0 = 3 * MLSTM_W
    b0 = a_cols
    c0 = b0 + B_W
    g0 = c0 + C_COLS
    w_a = _bf(jnp.concatenate([w[:, 0:if0], w[:, if0 + 2 * MLSTM_H:a_cols]], axis=1))
    w_b = _bf(w[:, b0:c0])
    gate_pad = jnp.zeros((D_MODEL, LANES - MLSTM_H), F32)
    w_c = _bf(jnp.concatenate(
        [w[:, c0:g0], w[:, if0:if0 + MLSTM_H], gate_pad,
         w[:, if0 + MLSTM_H:if0 + 2 * MLSTM_H], gate_pad,
         jnp.zeros((D_MODEL, LANES), F32)], axis=1))
    n_chunk = D_MODEL // MERGE_NC
    w_g = _bf(jnp.concatenate(
        [w[:, g0 + b * D_MODEL + j * MERGE_NC:g0 + b * D_MODEL + (j + 1) * MERGE_NC]
         for j in range(n_chunk) for b in range(3)], axis=1))
    return w_a, w_b, w_c, w_g


def _lane_pad(m):
    pad = [(0, 0)] * (m.ndim - 1) + [(0, LANES - m.shape[-1])]
    return jnp.expand_dims(jnp.pad(m.astype(F32), pad), -2)


def _stacked_params(lb_all, norm_pre, norm_post, mlstm_b_i, mlstm_b_f, mlstm_norm, hgrn_norm,
                    rwkv_mu, rwkv_w0, rwkv_w_up, rwkv_a0, rwkv_a_up, rwkv_k_k, rwkv_k_a, rwkv_r_k,
                    rwkv_gn_g, rwkv_gn_b, w_proj_a, w_proj_b, w_proj_c, w_out):
    row = lambda a: a.astype(F32)[:, None, :]
    zero = jnp.zeros(rwkv_w_up.shape, F32)
    wa_up = _bf(jnp.concatenate([jnp.concatenate([rwkv_w_up, zero], axis=2),
                                 jnp.concatenate([zero, rwkv_a_up], axis=2)], axis=1))
    return dict(
        norm_pre=row(norm_pre), norm_post=row(norm_post), bias_i=_lane_pad(mlstm_b_i),
        bias_f=_lane_pad(mlstm_b_f), mlstm_norm=row(mlstm_norm), lb=row(lb_all),
        hgrn_norm=row(hgrn_norm), mu=row(rwkv_mu), wa_up=wa_up, w0=row(rwkv_w0), a0=row(rwkv_a0),
        k_k=row(rwkv_k_k), k_a=row(rwkv_k_a), r_k=row(rwkv_r_k), gn_g=row(rwkv_gn_g),
        gn_b=row(rwkv_gn_b), wpa=_bf(w_proj_a), wpb=_bf(w_proj_b), wpc=_bf(w_proj_c),
        wo=_bf(w_out))


def _branches(p, layer, proj_main, proj_tail, st_s, *, bp, t_p, bs, t_s):
    tails = [(MLSTM_H, MLSTM_DH, MLSTM_DH), (MLSTM_H, MLSTM_DH), (1, LANES),
             (HGRN_H, HGRN_DK, HGRN_DV), (RWKV_H, RWKV_DH, RWKV_DH), (1, C_COLS)]
    st_zero = tuple(jnp.zeros((1, bp) + t, F32) for t in tails)
    y_meta, st_meta = _branch_call(*proj_tail, p, st_zero, layer, 0, n_seq=bp, n_chunks=1,
                                   length=N_META, row_off=0, name=f'branch_meta{layer}')
    y_main, st_p = _branch_call(*proj_main, p, tuple(s[None] for s in st_meta), layer, 0,
                                n_seq=bp, n_chunks=t_p // CHUNK, length=CHUNK, row_off=0,
                                name=f'branch_main{layer}')
    y_samp, st_so = _branch_call(*proj_tail, p, st_s, layer, layer, n_seq=bs, n_chunks=1,
                                 length=t_s, row_off=bp * N_META, name=f'branch_samp{layer}')
    y_tail = tuple(jnp.concatenate([a, b], axis=0) for a, b in zip(y_meta, y_samp))
    unpad = lambda st: (st[0], st[1], st[2][:, 0, 0:MLSTM_H]) + tuple(st[3:])
    return y_main, y_tail, unpad(st_p), unpad(st_so)


def kernel(x_prompt, x_sample, state_mlstm_C, state_mlstm_n, state_mlstm_m, state_hgrn_S,
           state_rwkv_S, cache_rwkv_shift, meta_tokens, norm_pre, norm_post, w_in,
           mlstm_b_i, mlstm_b_f, mlstm_norm, hgrn_lb_logits, hgrn_norm, rwkv_mu, rwkv_w0,
           rwkv_w_up, rwkv_a0, rwkv_a_up, rwkv_k_k, rwkv_k_a, rwkv_r_k, rwkv_gn_g, rwkv_gn_b,
           w_proj_a, w_proj_b, w_proj_c, w_out):
    bp, t_p, _ = x_prompt.shape
    bs, t_s, _ = x_sample.shape
    depth = w_in.shape[0]
    assert t_p % CHUNK == 0 and t_s % HGRN_SUB == 0 and (bp * N_META) % (BRANCH_GROUP * t_s) == 0

    sm = jax.nn.softmax(hgrn_lb_logits.astype(F32), axis=0)
    lb_all = jnp.cumsum(sm, axis=0) - sm[0]

    x_main = x_prompt.reshape(bp * t_p, D_MODEL)
    meta = jnp.broadcast_to(meta_tokens.astype(F32)[None], (bp, N_META, D_MODEL))
    x_tail = jnp.concatenate([meta.reshape(bp * N_META, D_MODEL),
                              x_sample.reshape(bs * t_s, D_MODEL)], axis=0)

    p = _stacked_params(lb_all, norm_pre, norm_post, mlstm_b_i, mlstm_b_f, mlstm_norm, hgrn_norm,
                        rwkv_mu, rwkv_w0, rwkv_w_up, rwkv_a0, rwkv_a_up, rwkv_k_k, rwkv_k_a,
                        rwkv_r_k, rwkv_gn_g, rwkv_gn_b, w_proj_a, w_proj_b, w_proj_c, w_out)
    st_s = (state_mlstm_C.astype(F32), state_mlstm_n.astype(F32), _lane_pad(state_mlstm_m),
            state_hgrn_S.astype(F32), state_rwkv_S.astype(F32), cache_rwkv_shift.astype(F32))

    outs_p, outs_s = [], []
    for l in range(depth):
        weights = _layer_weights(w_in[l])
        names = ('w_a', 'w_b', 'w_c')
        proj_main = tuple(_proj(x_main, p['norm_pre'], l, w, f'proj_{k}_main{l}')
                          for k, w in zip(names, weights[0:3]))
        proj_tail = tuple(_proj(x_tail, p['norm_pre'], l, w, f'proj_{k}_tail{l}')
                          for k, w in zip(names, weights[0:3]))
        y_main, y_tail, st_p_out, st_s_out = _branches(
            p, l, proj_main, proj_tail, st_s, bp=bp, t_p=t_p, bs=bs, t_s=t_s)
        merge_args = (weights[3], p['wpa'], p['wpb'], p['wpc'], p['wo'], p['norm_post'], l)
        x_main = _merge(x_main, p['norm_pre'], *y_main, *merge_args, f'merge_main{l}')
        x_tail = _merge(x_tail, p['norm_pre'], *y_tail, *merge_args, f'merge_tail{l}')
        outs_p.append(st_p_out)
        outs_s.append(st_s_out)

    states_p = tuple(jnp.stack([o[j] for o in outs_p]) for j in range(6))
    states_s = tuple(jnp.stack([o[j] for o in outs_s]) for j in range(6))
    y_prompt = x_main.reshape(bp, t_p, D_MODEL)
    y_sample = x_tail[bp * N_META:].reshape(bs, t_s, D_MODEL)
    return (y_prompt, y_sample) + states_p + states_s
```

```python
import functools
import math

import jax
import jax.numpy as jnp
from jax import lax
from jax.experimental import pallas as pl
from jax.experimental.pallas import tpu as pltpu

F32 = jnp.float32
BF16 = jnp.bfloat16

D_MODEL = 2048
CHUNK = 64
N_META = 16
EPS = 1e-6

MLSTM_H = 8
MLSTM_DH = 128
MLSTM_W = MLSTM_H * MLSTM_DH
HGRN_H = 4
HGRN_DK = 128
HGRN_DV = 128
HGRN_W = HGRN_H * HGRN_DV
RWKV_H = 8
RWKV_DH = 64
RWKV_W = RWKV_H * RWKV_DH
RWKV_RANK = 64
RWKV_GN_EPS = 64e-5
C_COLS = 4 * RWKV_W + 2 * RWKV_RANK

LANES = 128
HGRN_SUB = 8
LOG2E = 1.4426950408889634
A_W = 5 * MLSTM_W
B_W = 4 * HGRN_W
C_W = C_COLS + 3 * LANES
IF_BLOCK = C_COLS // LANES
MERGE_NC = 512
BRANCH_GROUP = 2
VMEM_LIMIT = 56 * 1024 * 1024


def _dot(a, b):
    return jnp.dot(a, b, preferred_element_type=F32)


def _dot_nt(a, b):
    return lax.dot_general(a, b, (((1,), (1,)), ((), ())), preferred_element_type=F32)


def _dot_tn(a, b):
    return lax.dot_general(a, b, (((0,), (0,)), ((), ())), preferred_element_type=F32)


def _bf(a):
    return a.astype(BF16)


def _sigmoid(x):
    return 1.0 / (1.0 + jnp.exp(-x))


def _silu(x):
    return x * _sigmoid(x)


def _softplus(x):
    return jnp.maximum(x, 0.0) + jnp.log1p(jnp.exp(-jnp.abs(x)))


def _tri(length, strict=False):
    row = lax.broadcasted_iota(jnp.int32, (length, length), 0)
    col = lax.broadcasted_iota(jnp.int32, (length, length), 1)
    return (row > col) if strict else (row >= col)


def _cumsum_time(x, tri_b):
    hi = _bf(x)
    r1 = x - hi.astype(F32)
    mid = _bf(r1)
    lo = _bf(r1 - mid.astype(F32))
    return _dot(tri_b, hi) + _dot(tri_b, mid) + _dot(tri_b, lo)


def _rmsnorm(x, g):
    return x * lax.rsqrt(jnp.mean(x * x, axis=-1, keepdims=True) + EPS) * g


def _row_tile(rows, cap):
    best = None
    for t in range(16, min(rows, cap) + 1, 16):
        if rows % t == 0:
            best = t
    assert best is not None, rows
    return best


def _col_tile(cols, cap):
    best = None
    for t in range(LANES, min(cols, cap) + 1, LANES):
        if cols % t == 0:
            best = t
    assert best is not None, cols
    return best


def _proj_kernel(x_ref, g_ref, w_ref, o_ref):
    h = _bf(_rmsnorm(x_ref[...], g_ref[...]))
    o_ref[...] = _dot(h, w_ref[...])


def _layer_spec(arr, layer):
    tail = arr.shape[1:]
    return pl.BlockSpec((None,) + tail, lambda *_: (layer,) + (0,) * len(tail))


def _proj(x, g, layer, w, name):
    rows, _ = x.shape
    cols = w.shape[1]
    tm = _row_tile(rows, 1024)
    tn = _col_tile(cols, 1280)
    return pl.pallas_call(
        _proj_kernel,
        grid=(cols // tn, rows // tm),
        in_specs=[
            pl.BlockSpec((tm, D_MODEL), lambda j, i: (i, 0)),
            _layer_spec(g, layer),
            pl.BlockSpec((D_MODEL, tn), lambda j, i: (0, j)),
        ],
        out_specs=pl.BlockSpec((tm, tn), lambda j, i: (i, j)),
        out_shape=jax.ShapeDtypeStruct((rows, cols), F32),
        compiler_params=pltpu.CompilerParams(
            dimension_semantics=("parallel", "parallel"), vmem_limit_bytes=VMEM_LIMIT),
        name=name,
    )(x, g, w)


def _merge_kernel(x_ref, gpre_ref, ya_ref, yb_ref, yc_ref, wg_ref,
                  wpa_ref, wpb_ref, wpc_ref, wo_ref, gpost_ref, o_ref, h_scr, acc_scr):
    j = pl.program_id(1)
    nc = MERGE_NC

    @pl.when(j == 0)
    def _():
        h_scr[...] = _bf(_rmsnorm(x_ref[...], gpre_ref[...]))
        acc_scr[...] = jnp.zeros_like(acc_scr)

    gates = _sigmoid(_dot(h_scr[...], wg_ref[...]))
    merged = (gates[:, 0:nc] * _dot(ya_ref[...], wpa_ref[...])
              + gates[:, nc:2 * nc] * _dot(yb_ref[...], wpb_ref[...])
              + gates[:, 2 * nc:3 * nc] * _dot(yc_ref[...], wpc_ref[...]))
    acc_scr[...] += _dot(_bf(merged), wo_ref[...])

    @pl.when(j == pl.num_programs(1) - 1)
    def _():
        o_ref[...] = x_ref[...] + _rmsnorm(acc_scr[...], gpost_ref[...])


def _merge(x, gpre, ya, yb, yc, wg, wpa, wpb, wpc, wo, gpost, layer, name):
    rows = x.shape[0]
    tm = _row_tile(rows, 640)
    nc = MERGE_NC
    n_chunk = D_MODEL // nc
    row = lambda i, j: (i, 0)
    return pl.pallas_call(
        _merge_kernel,
        grid=(rows // tm, n_chunk),
        in_specs=[
            pl.BlockSpec((tm, D_MODEL), row),
            _layer_spec(gpre, layer),
            pl.BlockSpec((tm, MLSTM_W), row),
            pl.BlockSpec((tm, HGRN_W), row),
            pl.BlockSpec((tm, RWKV_W), row),
            pl.BlockSpec((D_MODEL, 3 * nc), lambda i, j: (0, j)),
            pl.BlockSpec((None, MLSTM_W, nc), lambda i, j: (layer, 0, j)),
            pl.BlockSpec((None, HGRN_W, nc), lambda i, j: (layer, 0, j)),
            pl.BlockSpec((None, RWKV_W, nc), lambda i, j: (layer, 0, j)),
            pl.BlockSpec((None, nc, D_MODEL), lambda i, j: (layer, j, 0)),
            _layer_spec(gpost, layer),
        ],
        out_specs=pl.BlockSpec((tm, D_MODEL), row),
        out_shape=jax.ShapeDtypeStruct((rows, D_MODEL), F32),
        scratch_shapes=[pltpu.VMEM((tm, D_MODEL), BF16), pltpu.VMEM((tm, D_MODEL), F32)],
        compiler_params=pltpu.CompilerParams(
            dimension_semantics=("parallel", "arbitrary"), vmem_limit_bytes=VMEM_LIMIT),
        name=name,
    )(x, gpre, ya, yb, yc, wg, wpa, wpb, wpc, wo, gpost)


def _lane_mean(x, j_b):
    n = x.shape[0]
    hi = _bf(x)
    mid = _bf(x - hi.astype(F32))
    out = _dot(jnp.concatenate([hi, mid], axis=0), j_b)
    return out[0:n] + out[n:]


def _mlstm_stages(pa_ref, pc_ref, bias_i_ref, bias_f_ref, norm_ref, y_ref, cn_scr, m_scr, stores,
                  *, length):
    dh = MLSTM_DH
    heads = range(MLSTM_H)
    sls = [slice(h * dh, (h + 1) * dh) for h in heads]
    col = lambda part, h: slice(part * MLSTM_W + h * dh, part * MLSTM_W + (h + 1) * dh)
    cns = [cn_scr[h] for h in heads]
    m_prev = m_scr[...]

    causal = _tri(length)
    tri_b = _bf(jnp.where(causal, 1.0, 0.0))
    gate0 = IF_BLOCK * LANES
    ig = pc_ref[:, gate0:gate0 + LANES] + bias_i_ref[...]
    fg = pc_ref[:, gate0 + LANES:gate0 + 2 * LANES] + bias_f_ref[...]
    log_f = jnp.minimum(fg, 0.0) - jnp.log1p(jnp.exp(-jnp.abs(fg)))
    b_all = _cumsum_time(log_f, tri_b)
    yield
    c_all = ig - b_all
    row = lax.broadcasted_iota(jnp.int32, (length, LANES), 0)
    run_max = c_all
    shift = 1
    while shift < length:
        run_max = jnp.maximum(
            run_max, jnp.where(row >= shift, pltpu.roll(run_max, shift, 0), -jnp.inf))
        shift *= 2
    mx_all = jnp.maximum(run_max, m_prev)
    mt_all = b_all + mx_all
    mx_end = mx_all[length - 1:length]
    m_new = b_all[length - 1:length] + mx_end
    w_old_all = jnp.exp(m_prev - mx_end)
    c_t = c_all.T
    ones = jnp.ones((length, dh), F32)
    j_b = jnp.full((dh, dh), 1.0 / dh, BF16)
    scale = MLSTM_DH ** -0.5
    yield
    k_ts = [(pa_ref[:, col(1, h)] * scale).T for h in heads]
    qbs = [_bf(pa_ref[:, col(0, h)]) for h in heads]
    s_raw = [_dot(qbs[h], _bf(k_ts[h])) for h in heads]
    yield
    inter = [_dot(qbs[h], _bf(cns[h])) for h in heads]
    yield
    mx_bs = [jnp.broadcast_to(mx_all[:, h:h + 1], (length, dh)) for h in heads]
    es = [jnp.exp(jnp.where(causal, c_t[h:h + 1, :] - mx_bs[h][:, 0:length], -jnp.inf))
          for h in heads]
    yield
    v1s = [_bf(jnp.concatenate([pa_ref[:, col(2, h)], ones], axis=1)) for h in heads]
    tots = []
    for h in heads:
        w_inter = jnp.exp(m_prev[:, h:h + 1] - mx_bs[h])
        tots.append(jnp.concatenate([w_inter, w_inter], axis=1) * inter[h]
                    + _dot(_bf(s_raw[h] * es[h]), v1s[h]))
    yield
    new_cn = [w_old_all[:, h:h + 1] * cns[h]
              + _dot(_bf(k_ts[h] * es[h][length - 1:length, :]), v1s[h]) for h in heads]
    yield
    cens = [tots[h][:, 0:dh] - _lane_mean(tots[h][:, 0:dh], j_b) for h in heads]
    yield
    vars_ = [_lane_mean(cens[h] * cens[h], j_b) for h in heads]
    yield
    outs = []
    for h in heads:
        mt_b = jnp.broadcast_to(mt_all[:, h:h + 1], (length, dh))
        inv = 1.0 / jnp.maximum(jnp.abs(tots[h][:, dh:]), jnp.exp(-mt_b))
        hid = cens[h] * inv * lax.rsqrt(vars_[h] * inv * inv + EPS) * norm_ref[:, sls[h]]
        outs.append(_bf(hid * _sigmoid(pa_ref[:, col(3, h)]) * _silu(pa_ref[:, col(4, h)])))

    def store():
        y_ref[...] = jnp.concatenate(outs, axis=1)
        for h in heads:
            cn_scr[h] = new_cn[h]
        m_scr[...] = m_new

    stores.append(store)


def _hgrn_stages(pb_ref, lb_ref, norm_ref, y_ref, st_scr, stores, *, length):
    dk = HGRN_DK
    sub = min(HGRN_SUB, length)
    n_sub = length // sub
    heads = range(HGRN_H)
    sls = [slice(h * dk, (h + 1) * dk) for h in heads]
    col = lambda part, h: slice(part * HGRN_W + h * dk, part * HGRN_W + (h + 1) * dk)
    sts = [st_scr[h] for h in heads]

    lb = lb_ref[...]
    fp = pb_ref[:, HGRN_W:2 * HGRN_W]
    f_gate = lb + (1.0 - lb) * _sigmoid(fp)
    k_all = (1.0 - lb) * _sigmoid(-fp)
    tri_b = _bf(jnp.where(_tri(length), 1.0, 0.0))
    a_all = _cumsum_time(jnp.log(f_gate), tri_b)
    lane_l = lax.broadcasted_iota(jnp.int32, (sub, length), 1)
    row_l = lax.broadcasted_iota(jnp.int32, (sub, length), 0)
    yield
    qs = [pb_ref[:, col(0, h)] for h in heads]
    ks = [k_all[:, sl] for sl in sls]
    avs = [a_all[:, sl] for sl in sls]
    a2_all = a_all * LOG2E
    a2s = [a2_all[:, sl] for sl in sls]
    vbs = [_bf(pb_ref[:, col(2, h)]) for h in heads]
    o_inter = [_dot_nt(_bf(qs[h] * jnp.exp(avs[h])), _bf(sts[h])) for h in heads]
    yield
    new_st = []
    for h in heads:
        a_end = avs[h][length - 1:length]
        new_st.append(jnp.exp(a_end) * sts[h]
                      + _dot_tn(vbs[h], _bf(ks[h] * jnp.exp(a_end - avs[h]))))
    yield

    blocks = [[] for _ in heads]
    for i in range(n_sub):
        r0 = i * sub
        blks = []
        for h in heads:
            if i > 0:
                ref_row = avs[h][r0:r0 + 1]
                q_s = _bf(qs[h][r0:r0 + sub] * jnp.exp(avs[h][r0:r0 + sub] - ref_row))
                k_s = jnp.concatenate(
                    [ks[h][0:r0] * jnp.exp(ref_row - avs[h][0:r0]),
                     jnp.zeros((length - r0, dk), F32)], axis=0)
                blks.append(_dot_nt(q_s, _bf(k_s)))
            else:
                blks.append(jnp.zeros((sub, length), F32))
        for s_idx in range(sub):
            for h in heads:
                a_i = a2s[h][r0:r0 + sub]
                e = jnp.exp2(jnp.minimum(a_i - a_i[s_idx:s_idx + 1], 0.0))
                col_v = jnp.sum(qs[h][r0:r0 + sub] * e * ks[h][r0 + s_idx:r0 + s_idx + 1],
                                axis=-1, keepdims=True)
                blks[h] = jnp.where(lane_l == r0 + s_idx, col_v, blks[h])
        yield
        for h in heads:
            blocks[h].append(jnp.where(lane_l <= r0 + row_l, blks[h], 0.0))

    outs = []
    for h in heads:
        scores = jnp.concatenate(blocks[h], axis=0) if n_sub > 1 else blocks[h][0]
        o = o_inter[h] + _dot(_bf(scores), vbs[h])
        o = o * lax.rsqrt(jnp.mean(o * o, axis=-1, keepdims=True) + EPS)
        outs.append(_bf(o * norm_ref[:, sls[h]] * _silu(pb_ref[:, col(3, h)])))

    def store():
        y_ref[...] = jnp.concatenate(outs, axis=1)
        for h in heads:
            st_scr[h] = new_st[h]

    stores.append(store)


def _pair_sum(x, even):
    s_even = jnp.sum(jnp.where(even, x, 0.0), axis=-1, keepdims=True)
    s_odd = jnp.sum(jnp.where(even, 0.0, x), axis=-1, keepdims=True)
    return jnp.where(even, s_even, s_odd)


def _rwkv_stages(pc_ref, mu_ref, wa_up_ref, w0_ref, a0_ref, kk_ref, ka_ref, rk_ref, gg_ref, gb_ref,
                 y_ref, bd_scr, carry_scr, stores, *, length):
    dh = RWKV_DH
    w = RWKV_W
    pairs = range(RWKV_H // 2)
    sls = [slice(p * LANES, (p + 1) * LANES) for p in pairs]
    l2 = 2 * length
    bds = [bd_scr[p] for p in pairs]
    gn_g = gg_ref[...]
    gn_b = gb_ref[...]

    pc = pc_ref[:, 0:C_COLS]
    row0 = lax.broadcasted_iota(jnp.int32, (length, C_COLS), 0) == 0
    prev = jnp.where(row0, carry_scr[...], pltpu.roll(pc, 1, 0))
    xs = pc + mu_ref[...] * (prev - pc)
    r = xs[:, 0:w]
    k = xs[:, w:2 * w]
    v = xs[:, 2 * w:3 * w]
    low = xs[:, 3 * w:3 * w + 2 * RWKV_RANK]
    z = xs[:, 3 * w + 2 * RWKV_RANK:]
    lane = lax.broadcasted_iota(jnp.int32, (length, 2 * RWKV_RANK), 1)
    low = jnp.where(lane < RWKV_RANK, jnp.tanh(low), low)
    up = _dot(_bf(low), wa_up_ref[...])
    yield
    w_logit = -_softplus(-(w0_ref[...] + up[:, 0:w])) - 0.5
    log_w = -jnp.exp(w_logit)
    a = _sigmoid(a0_ref[...] + up[:, w:2 * w])
    kk_raw = k * kk_ref[...]
    k2 = k * (1.0 + (a - 1.0) * ka_ref[...])
    rk_bonus = r * k2 * rk_ref[...]
    tri_b = _bf(jnp.where(_tri(length), 1.0, 0.0))
    cum = _cumsum_time(log_w, tri_b)
    yield
    p_in = jnp.exp(cum)
    p_ex = jnp.exp(cum - log_w)
    p_inv = jnp.exp(-cum)
    cum_end = cum[length - 1:length]
    p_end = jnp.exp(cum_end - cum)
    dec_end = jnp.exp(cum_end)

    even = lax.broadcasted_iota(jnp.int32, (length, LANES), 1) < dh
    rows2 = lax.broadcasted_iota(jnp.int32, (l2, l2), 0)
    cols2 = lax.broadcasted_iota(jnp.int32, (l2, l2), 1)
    same = (rows2 >= length) == (cols2 >= length)
    strict_bd = same & (rows2 > cols2)
    incl_bd = same & (rows2 >= cols2)
    rows_s = lax.broadcasted_iota(jnp.int32, (LANES, LANES), 0)
    cols_s = lax.broadcasted_iota(jnp.int32, (LANES, LANES), 1)
    state_bd = (rows_s >= dh) == (cols_s >= dh)
    n_double = int(math.log2(length))
    assert 2 ** n_double == length

    def stack(x):
        return jnp.concatenate([jnp.where(even, x, 0.0), jnp.where(even, 0.0, x)], axis=0)

    kas, lhss, grams = [], [], []
    for p in pairs:
        sl = sls[p]
        kk_r = kk_raw[:, sl]
        kk = kk_r / jnp.maximum(jnp.sqrt(_pair_sum(kk_r * kk_r, even)), 1e-12)
        ka = kk * a[:, sl]
        lhs = _bf(jnp.concatenate([stack(-kk * p_ex[:, sl]), stack(r[:, sl] * p_in[:, sl])], axis=0))
        bh = _bf(ka * p_inv[:, sl])
        kh = _bf(k2[:, sl] * p_inv[:, sl])
        rhs = jnp.concatenate([bh, bh, kh, kh], axis=0)
        grams.append(_dot_nt(lhs, rhs))
        kas.append(ka)
        lhss.append(lhs)
    yield
    uy0s = [_dot_nt(lhss[p], _bf(bds[p])) for p in pairs]
    v_stacks = [_bf(stack(v[:, sl])) for sl in sls]
    yield
    n_mats = [jnp.where(strict_bd, grams[p][0:l2, 0:l2], 0.0) for p in pairs]
    xs_u = [uy0s[p][0:l2] + _dot(_bf(jnp.where(strict_bd, grams[p][0:l2, l2:], 0.0)), v_stacks[p])
            for p in pairs]
    abks = [_bf(jnp.concatenate([jnp.where(incl_bd, grams[p][l2:, 0:l2], 0.0),
                                 jnp.where(incl_bd, grams[p][l2:, l2:], 0.0)], axis=1))
            for p in pairs]
    yield
    for step in range(n_double):
        for p in pairs:
            n_b = _bf(n_mats[p])
            xs_u[p] = xs_u[p] + _dot(n_b, _bf(xs_u[p]))
            if step + 1 < n_double:
                n_mats[p] = _dot(n_b, n_b)
        yield

    new_bd, ys = [], []
    for p in pairs:
        sl = sls[p]
        u_st = xs_u[p]
        y_st = uy0s[p][l2:] + _dot(abks[p], jnp.concatenate([_bf(u_st), v_stacks[p]], axis=0))
        ys.append(y_st[0:length] + y_st[length:])
        u_pair = u_st[0:length] + u_st[length:]
        upd = _dot_tn(_bf(jnp.concatenate([u_pair, v[:, sl]], axis=0)),
                      _bf(jnp.concatenate([kas[p] * p_end[:, sl], k2[:, sl] * p_end[:, sl]], axis=0)))
        new_bd.append(bds[p] * dec_end[:, sl] + jnp.where(state_bd, upd, 0.0))
    yield
    outs = []
    for p in pairs:
        sl = sls[p]
        mu = _pair_sum(ys[p], even) * (1.0 / dh)
        cen = ys[p] - mu
        var = _pair_sum(cen * cen, even) * (1.0 / dh)
        y = cen * lax.rsqrt(var + RWKV_GN_EPS) * gn_g[:, sl] + gn_b[:, sl]
        bonus = _pair_sum(rk_bonus[:, sl], even) * v[:, sl]
        outs.append(_bf((y + bonus) * _silu(z[:, sl])))

    def store():
        y_ref[...] = jnp.concatenate(outs, axis=1)
        carry_scr[...] = pc[length - 1:length, :]
        for p in pairs:
            bd_scr[p] = new_bd[p]

    stores.append(store)


def _branch_kernel(pa_ref, pb_ref, pc_ref, bias_i_ref, bias_f_ref, anorm_ref, lb_ref, hnorm_ref,
                   mu_ref, wa_up_ref, w0_ref, a0_ref, kk_ref, ka_ref, rk_ref, gg_ref, gb_ref,
                   c0_ref, n0_ref, m0_ref, hs0_ref, rs0_ref, shift0_ref,
                   ya_ref, yb_ref, yc_ref, c_out, n_out, m_out, hs_out, rs_out, shift_out,
                   cn_scr, m_scr, st_scr, bd_scr, carry_scr, *, length, n_grp):
    c_idx = pl.program_id(1)
    dh = MLSTM_DH
    rh = RWKV_DH
    streams = range(n_grp)

    def rows(ref, g):
        return ref.at[g] if len(ref.shape) == 3 else ref.at[pl.ds(g * length, length)]

    @pl.when(c_idx == 0)
    def _():
        for g in streams:
            n_t = n0_ref[g].T
            for h in range(MLSTM_H):
                cn_scr[g, h, :, 0:dh] = c0_ref[g, h].T
                cn_scr[g, h, :, dh:2 * dh] = jnp.broadcast_to(n_t[:, h:h + 1], (dh, dh))
            m_scr[g] = m0_ref[g]
            for h in range(HGRN_H):
                st_scr[g, h] = hs0_ref[g, h].T
            zero = jnp.zeros((rh, rh), F32)
            for p in range(RWKV_H // 2):
                top = jnp.concatenate([rs0_ref[g, 2 * p], zero], axis=1)
                bot = jnp.concatenate([zero, rs0_ref[g, 2 * p + 1]], axis=1)
                bd_scr[g, p] = jnp.concatenate([top, bot], axis=0)
            carry_scr[g] = shift0_ref[g]

    stores = []
    active = (
        [_rwkv_stages(rows(pc_ref, g), mu_ref, wa_up_ref, w0_ref, a0_ref, kk_ref, ka_ref, rk_ref,
                      gg_ref, gb_ref, yc_ref.at[g], bd_scr.at[g], carry_scr.at[g], stores,
                      length=length) for g in streams]
        + [_mlstm_stages(rows(pa_ref, g), rows(pc_ref, g), bias_i_ref, bias_f_ref, anorm_ref,
                         ya_ref.at[g], cn_scr.at[g], m_scr.at[g], stores, length=length)
           for g in streams]
        + [_hgrn_stages(rows(pb_ref, g), lb_ref, hnorm_ref, yb_ref.at[g], st_scr.at[g], stores,
                        length=length) for g in streams])
    while active:
        for gen in list(active):
            if next(gen, StopIteration) is StopIteration:
                active.remove(gen)
    for store in stores:
        store()

    @pl.when(c_idx == pl.num_programs(1) - 1)
    def _():
        for g in streams:
            for h in range(MLSTM_H):
                c_out[g, h] = cn_scr[g, h, :, 0:dh].T
                n_out[g, h:h + 1, :] = cn_scr[g, h, :, dh:2 * dh].T[0:1, :]
            m_out[g] = m_scr[g]
            for h in range(HGRN_H):
                hs_out[g, h] = st_scr[g, h].T
            for p in range(RWKV_H // 2):
                rs_out[g, 2 * p] = bd_scr[g, p, 0:rh, 0:rh]
                rs_out[g, 2 * p + 1] = bd_scr[g, p, rh:2 * rh, rh:2 * rh]
            shift_out[g] = carry_scr[g]


def _branch_call(pa, pb, pc, p, states, layer, state_layer, *, n_seq, n_chunks, length, row_off,
                 name):
    grp = BRANCH_GROUP
    assert n_seq % grp == 0
    if n_chunks == 1:
        assert row_off % (grp * length) == 0
        blk0 = row_off // (grp * length)
        views = (pa, pb, pc)
        row_spec = lambda width: pl.BlockSpec((grp * length, width), lambda b, c: (blk0 + b, 0))
    else:
        assert row_off == 0 and pa.shape[0] == n_seq * n_chunks * length
        views = tuple(a.reshape(n_seq, n_chunks * length, a.shape[1]) for a in (pa, pb, pc))
        row_spec = lambda width: pl.BlockSpec((grp, length, width), lambda b, c: (b, c, 0))
    out_spec = lambda width: pl.BlockSpec((grp, length, width), lambda b, c: (b, c, 0))
    state_tails = [(MLSTM_H, MLSTM_DH, MLSTM_DH), (MLSTM_H, MLSTM_DH), (1, LANES),
                   (HGRN_H, HGRN_DK, HGRN_DV), (RWKV_H, RWKV_DH, RWKV_DH), (1, C_COLS)]
    state_in = [pl.BlockSpec((None, grp) + t, lambda b, c, t=t: (state_layer, b) + (0,) * len(t))
                for t in state_tails]
    state_out = [pl.BlockSpec((grp,) + t, lambda b, c, t=t: (b,) + (0,) * len(t))
                 for t in state_tails]
    state_shapes = [jax.ShapeDtypeStruct((n_seq,) + t, F32) for t in state_tails]
    params = [p['bias_i'], p['bias_f'], p['mlstm_norm'], p['lb'], p['hgrn_norm'], p['mu'],
              p['wa_up'], p['w0'], p['a0'], p['k_k'], p['k_a'], p['r_k'], p['gn_g'], p['gn_b']]
    t_len = n_chunks * length
    outs = pl.pallas_call(
        functools.partial(_branch_kernel, length=length, n_grp=grp),
        grid=(n_seq // grp, n_chunks),
        in_specs=[row_spec(A_W), row_spec(B_W), row_spec(C_W)]
        + [_layer_spec(a, layer) for a in params] + state_in,
        out_specs=[out_spec(MLSTM_W), out_spec(HGRN_W), out_spec(RWKV_W)] + state_out,
        out_shape=[jax.ShapeDtypeStruct((n_seq, t_len, MLSTM_W), BF16),
                   jax.ShapeDtypeStruct((n_seq, t_len, HGRN_W), BF16),
                   jax.ShapeDtypeStruct((n_seq, t_len, RWKV_W), BF16)] + state_shapes,
        scratch_shapes=[pltpu.VMEM((grp, MLSTM_H, MLSTM_DH, 2 * MLSTM_DH), F32),
                        pltpu.VMEM((grp, 1, LANES), F32),
                        pltpu.VMEM((grp, HGRN_H, HGRN_DV, HGRN_DK), F32),
                        pltpu.VMEM((grp, RWKV_H // 2, 2 * RWKV_DH, 2 * RWKV_DH), F32),
                        pltpu.VMEM((grp, 1, C_COLS), F32)],
        compiler_params=pltpu.CompilerParams(
            dimension_semantics=("parallel", "arbitrary"), vmem_limit_bytes=VMEM_LIMIT),
        name=name,
    )(*views, *params, *states)
    ys = tuple(y.reshape(n_seq * t_len, y.shape[2]) for y in outs[0:3])
    return ys, outs[3:]


def _skew_kernel(main_ref, next_ref, o_ref, *, shift, copy_below):
    width = o_ref.shape[1]

    def skewed():
        x = jnp.concatenate([main_ref[...], next_ref[...]], axis=1)
        o_ref[...] = _bf(x[:, shift:shift + width])

    if copy_below == 0:
        skewed()
    else:
        t = pl.program_id(0)

        @pl.when(t < copy_below)
        def _():
            o_ref[...] = _bf(main_ref[...])

        @pl.when(t >= copy_below)
        def _():
            skewed()


def _skew_copy(w_in, layer, n_blocks, width, next_width, main_idx, next_idx, shift, copy_below,
               name):
    return pl.pallas_call(
        functools.partial(_skew_kernel, shift=shift, copy_below=copy_below),
        grid=(n_blocks,),
        in_specs=[pl.BlockSpec((None, D_MODEL, width), lambda t: (layer, 0, main_idx(t))),
                  pl.BlockSpec((None, D_MODEL, next_width), lambda t: (layer, 0, next_idx(t)))],
        out_specs=pl.BlockSpec((D_MODEL, width), lambda t: (0, t)),
        out_shape=jax.ShapeDtypeStruct((D_MODEL, n_blocks * width), BF16),
        compiler_params=pltpu.CompilerParams(
            dimension_semantics=("parallel",), vmem_limit_bytes=VMEM_LIMIT),
        name=name,
    )(w_in, w_in)


def _layer_weights(w_in, layer):
    skew = 2 * MLSTM_H
    blk = 4 * LANES
    if0 = 3 * MLSTM_W
    b0 = (if0 + 2 * MLSTM_W) // blk
    c0 = (if0 + 2 * MLSTM_W + B_W) // LANES
    g0 = if0 + 2 * MLSTM_W + B_W + C_COLS
    n_chunk = D_MODEL // MERGE_NC
    assert MERGE_NC == blk and g0 % blk + skew <= 2 * LANES
    w_a = _skew_copy(w_in, layer, A_W // blk, blk, LANES, lambda t: t, lambda t: 4 * (t + 1),
                     skew, if0 // blk, f'regroup_a{layer}')
    w_b = _skew_copy(w_in, layer, B_W // blk, blk, LANES, lambda t: b0 + t,
                     lambda t: 4 * (b0 + t + 1), skew, 0, f'regroup_b{layer}')
    w_c_main = _skew_copy(w_in, layer, C_COLS // LANES, LANES, LANES, lambda t: c0 + t,
                          lambda t: c0 + t + 1, skew, 0, f'regroup_c{layer}')
    g_main = lambda t: g0 // blk + (t % 3) * (D_MODEL // blk) + t // 3
    w_g = _skew_copy(w_in, layer, 3 * n_chunk, blk, 2 * LANES, g_main,
                     lambda t: 2 * (g_main(t) + 1), g0 % blk + skew, 0, f'regroup_g{layer}')
    w = w_in[layer]
    gate_pad = jnp.zeros((D_MODEL, LANES - MLSTM_H), BF16)
    w_c = jnp.concatenate(
        [w_c_main, _bf(w[:, if0:if0 + MLSTM_H]), gate_pad,
         _bf(w[:, if0 + MLSTM_H:if0 + 2 * MLSTM_H]), gate_pad,
         jnp.zeros((D_MODEL, LANES), BF16)], axis=1)
    return w_a, w_b, w_c, w_g


def _lane_pad(m):
    pad = [(0, 0)] * (m.ndim - 1) + [(0, LANES - m.shape[-1])]
    return jnp.expand_dims(jnp.pad(m.astype(F32), pad), -2)


def _stacked_params(lb_all, norm_pre, norm_post, mlstm_b_i, mlstm_b_f, mlstm_norm, hgrn_norm,
                    rwkv_mu, rwkv_w0, rwkv_w_up, rwkv_a0, rwkv_a_up, rwkv_k_k, rwkv_k_a, rwkv_r_k,
                    rwkv_gn_g, rwkv_gn_b, w_proj_a, w_proj_b, w_proj_c, w_out):
    row = lambda a: a.astype(F32)[:, None, :]
    zero = jnp.zeros(rwkv_w_up.shape, F32)
    wa_up = _bf(jnp.concatenate([jnp.concatenate([rwkv_w_up, zero], axis=2),
                                 jnp.concatenate([zero, rwkv_a_up], axis=2)], axis=1))
    return dict(
        norm_pre=row(norm_pre), norm_post=row(norm_post), bias_i=_lane_pad(mlstm_b_i),
        bias_f=_lane_pad(mlstm_b_f), mlstm_norm=row(mlstm_norm), lb=row(lb_all),
        hgrn_norm=row(hgrn_norm), mu=row(rwkv_mu), wa_up=wa_up, w0=row(rwkv_w0), a0=row(rwkv_a0),
        k_k=row(rwkv_k_k), k_a=row(rwkv_k_a), r_k=row(rwkv_r_k), gn_g=row(rwkv_gn_g),
        gn_b=row(rwkv_gn_b), wpa=_bf(w_proj_a), wpb=_bf(w_proj_b), wpc=_bf(w_proj_c),
        wo=_bf(w_out))


def _branches(p, layer, proj_main, proj_tail, st_s, *, bp, t_p, bs, t_s):
    tails = [(MLSTM_H, MLSTM_DH, MLSTM_DH), (MLSTM_H, MLSTM_DH), (1, LANES),
             (HGRN_H, HGRN_DK, HGRN_DV), (RWKV_H, RWKV_DH, RWKV_DH), (1, C_COLS)]
    st_zero = tuple(jnp.zeros((1, bp) + t, F32) for t in tails)
    y_meta, st_meta = _branch_call(*proj_tail, p, st_zero, layer, 0, n_seq=bp, n_chunks=1,
                                   length=N_META, row_off=bs * t_s, name=f'branch_meta{layer}')
    y_main, st_p = _branch_call(*proj_main, p, tuple(s[None] for s in st_meta), layer, 0,
                                n_seq=bp, n_chunks=t_p // CHUNK, length=CHUNK, row_off=0,
                                name=f'branch_main{layer}')
    y_samp, st_so = _branch_call(*proj_tail, p, st_s, layer, layer, n_seq=bs, n_chunks=1,
                                 length=t_s, row_off=0, name=f'branch_samp{layer}')
    y_tail = tuple(jnp.concatenate([a, b], axis=0) for a, b in zip(y_samp, y_meta))
    unpad = lambda st: (st[0], st[1], st[2][:, 0, 0:MLSTM_H]) + tuple(st[3:])
    return y_main, y_tail, unpad(st_p), unpad(st_so)


def kernel(x_prompt, x_sample, state_mlstm_C, state_mlstm_n, state_mlstm_m, state_hgrn_S,
           state_rwkv_S, cache_rwkv_shift, meta_tokens, norm_pre, norm_post, w_in,
           mlstm_b_i, mlstm_b_f, mlstm_norm, hgrn_lb_logits, hgrn_norm, rwkv_mu, rwkv_w0,
           rwkv_w_up, rwkv_a0, rwkv_a_up, rwkv_k_k, rwkv_k_a, rwkv_r_k, rwkv_gn_g, rwkv_gn_b,
           w_proj_a, w_proj_b, w_proj_c, w_out):
    bp, t_p, _ = x_prompt.shape
    bs, t_s, _ = x_sample.shape
    depth = w_in.shape[0]
    assert t_p % CHUNK == 0 and t_s % HGRN_SUB == 0 and (bs * t_s) % (BRANCH_GROUP * N_META) == 0

    sm = jax.nn.softmax(hgrn_lb_logits.astype(F32), axis=0)
    lb_all = jnp.cumsum(sm, axis=0) - sm[0]

    x_main = x_prompt.reshape(bp * t_p, D_MODEL)
    meta = jnp.broadcast_to(meta_tokens.astype(F32)[None], (bp, N_META, D_MODEL))
    x_tail = jnp.concatenate([x_sample.reshape(bs * t_s, D_MODEL),
                              meta.reshape(bp * N_META, D_MODEL)], axis=0)

    p = _stacked_params(lb_all, norm_pre, norm_post, mlstm_b_i, mlstm_b_f, mlstm_norm, hgrn_norm,
                        rwkv_mu, rwkv_w0, rwkv_w_up, rwkv_a0, rwkv_a_up, rwkv_k_k, rwkv_k_a,
                        rwkv_r_k, rwkv_gn_g, rwkv_gn_b, w_proj_a, w_proj_b, w_proj_c, w_out)
    st_s = (state_mlstm_C.astype(F32), state_mlstm_n.astype(F32), _lane_pad(state_mlstm_m),
            state_hgrn_S.astype(F32), state_rwkv_S.astype(F32), cache_rwkv_shift.astype(F32))

    outs_p, outs_s = [], []
    for l in range(depth):
        weights = _layer_weights(w_in, l)
        names = ('w_a', 'w_b', 'w_c')
        proj_main = tuple(_proj(x_main, p['norm_pre'], l, w, f'proj_{k}_main{l}')
                          for k, w in zip(names, weights[0:3]))
        proj_tail = tuple(_proj(x_tail, p['norm_pre'], l, w, f'proj_{k}_tail{l}')
                          for k, w in zip(names, weights[0:3]))
        y_main, y_tail, st_p_out, st_s_out = _branches(
            p, l, proj_main, proj_tail, st_s, bp=bp, t_p=t_p, bs=bs, t_s=t_s)
        merge_args = (weights[3], p['wpa'], p['wpb'], p['wpc'], p['wo'], p['norm_post'], l)
        x_main = _merge(x_main, p['norm_pre'], *y_main, *merge_args, f'merge_main{l}')
        x_tail = _merge(x_tail, p['norm_pre'], *y_tail, *merge_args, f'merge_tail{l}')
        outs_p.append(st_p_out)
        outs_s.append(st_s_out)

    states_p = tuple(jnp.stack([o[j] for o in outs_p]) for j in range(6))
    states_s = tuple(jnp.stack([o[j] for o in outs_s]) for j in range(6))
    y_prompt = x_main.reshape(bp, t_p, D_MODEL)
    y_sample = x_tail[0:bs * t_s].reshape(bs, t_s, D_MODEL)
    return (y_prompt, y_sample) + states_p + states_s
```

```python
import functools
import math

import jax
import jax.numpy as jnp
from jax import lax
from jax.experimental import pallas as pl
from jax.experimental.pallas import tpu as pltpu

F32 = jnp.float32
BF16 = jnp.bfloat16

D_MODEL = 2048
CHUNK = 64
N_META = 16
EPS = 1e-6

MLSTM_H = 8
MLSTM_DH = 128
MLSTM_W = MLSTM_H * MLSTM_DH
HGRN_H = 4
HGRN_DK = 128
HGRN_DV = 128
HGRN_W = HGRN_H * HGRN_DV
RWKV_H = 8
RWKV_DH = 64
RWKV_W = RWKV_H * RWKV_DH
RWKV_RANK = 64
RWKV_GN_EPS = 64e-5
C_COLS = 4 * RWKV_W + 2 * RWKV_RANK

LANES = 128
HGRN_SUB = 8
LOG2E = 1.4426950408889634
A_W = 5 * MLSTM_W
B_W = 4 * HGRN_W
C_W = C_COLS + 3 * LANES
IF_BLOCK = C_COLS // LANES
MERGE_NC = 512
BRANCH_GROUP = 2
VMEM_LIMIT = 56 * 1024 * 1024


def _dot(a, b):
    return jnp.dot(a, b, preferred_element_type=F32)


def _dot_nt(a, b):
    return lax.dot_general(a, b, (((1,), (1,)), ((), ())), preferred_element_type=F32)


def _dot_tn(a, b):
    return lax.dot_general(a, b, (((0,), (0,)), ((), ())), preferred_element_type=F32)


def _bf(a):
    return a.astype(BF16)


def _sigmoid(x):
    return 1.0 / (1.0 + jnp.exp(-x))


def _silu(x):
    return x * _sigmoid(x)


def _softplus(x):
    return jnp.maximum(x, 0.0) + jnp.log1p(jnp.exp(-jnp.abs(x)))


def _tri(length, strict=False):
    row = lax.broadcasted_iota(jnp.int32, (length, length), 0)
    col = lax.broadcasted_iota(jnp.int32, (length, length), 1)
    return (row > col) if strict else (row >= col)


def _cumsum_time(x, tri_b):
    hi = _bf(x)
    r1 = x - hi.astype(F32)
    mid = _bf(r1)
    lo = _bf(r1 - mid.astype(F32))
    return _dot(tri_b, hi) + _dot(tri_b, mid) + _dot(tri_b, lo)


def _rmsnorm(x, g):
    return x * lax.rsqrt(jnp.mean(x * x, axis=-1, keepdims=True) + EPS) * g


def _row_tile(rows, cap):
    best = None
    for t in range(16, min(rows, cap) + 1, 16):
        if rows % t == 0:
            best = t
    assert best is not None, rows
    return best


def _col_tile(cols, cap):
    best = None
    for t in range(LANES, min(cols, cap) + 1, LANES):
        if cols % t == 0:
            best = t
    assert best is not None, cols
    return best


def _proj_kernel(x_ref, g_ref, w_ref, o_ref):
    h = _bf(_rmsnorm(x_ref[...], g_ref[...]))
    o_ref[...] = _dot_nt(h, w_ref[...])


def _layer_spec(arr, layer):
    tail = arr.shape[1:]
    return pl.BlockSpec((None,) + tail, lambda *_: (layer,) + (0,) * len(tail))


def _proj(x, g, layer, w, name):
    rows, _ = x.shape
    cols = w.shape[0]
    tm = _row_tile(rows, 1024)
    tn = _col_tile(cols, 1280)
    return pl.pallas_call(
        _proj_kernel,
        grid=(cols // tn, rows // tm),
        in_specs=[
            pl.BlockSpec((tm, D_MODEL), lambda j, i: (i, 0)),
            _layer_spec(g, layer),
            pl.BlockSpec((tn, D_MODEL), lambda j, i: (j, 0)),
        ],
        out_specs=pl.BlockSpec((tm, tn), lambda j, i: (i, j)),
        out_shape=jax.ShapeDtypeStruct((rows, cols), F32),
        compiler_params=pltpu.CompilerParams(
            dimension_semantics=("parallel", "parallel"), vmem_limit_bytes=VMEM_LIMIT),
        name=name,
    )(x, g, w)


def _merge_kernel(x_ref, gpre_ref, ya_ref, yb_ref, yc_ref, wg_ref,
                  wpa_ref, wpb_ref, wpc_ref, wo_ref, gpost_ref, o_ref, h_scr, acc_scr):
    j = pl.program_id(1)
    nc = MERGE_NC

    @pl.when(j == 0)
    def _():
        h_scr[...] = _bf(_rmsnorm(x_ref[...], gpre_ref[...]))
        acc_scr[...] = jnp.zeros_like(acc_scr)

    gates = _sigmoid(_dot_nt(h_scr[...], wg_ref[...]))
    merged = (gates[:, 0:nc] * _dot(ya_ref[...], wpa_ref[...])
              + gates[:, nc:2 * nc] * _dot(yb_ref[...], wpb_ref[...])
              + gates[:, 2 * nc:3 * nc] * _dot(yc_ref[...], wpc_ref[...]))
    acc_scr[...] += _dot(_bf(merged), wo_ref[...])

    @pl.when(j == pl.num_programs(1) - 1)
    def _():
        o_ref[...] = x_ref[...] + _rmsnorm(acc_scr[...], gpost_ref[...])


def _merge(x, gpre, ya, yb, yc, wg, wpa, wpb, wpc, wo, gpost, layer, name):
    rows = x.shape[0]
    tm = _row_tile(rows, 640)
    nc = MERGE_NC
    n_chunk = D_MODEL // nc
    row = lambda i, j: (i, 0)
    return pl.pallas_call(
        _merge_kernel,
        grid=(rows // tm, n_chunk),
        in_specs=[
            pl.BlockSpec((tm, D_MODEL), row),
            _layer_spec(gpre, layer),
            pl.BlockSpec((tm, MLSTM_W), row),
            pl.BlockSpec((tm, HGRN_W), row),
            pl.BlockSpec((tm, RWKV_W), row),
            pl.BlockSpec((3 * nc, D_MODEL), lambda i, j: (j, 0)),
            pl.BlockSpec((None, MLSTM_W, nc), lambda i, j: (layer, 0, j)),
            pl.BlockSpec((None, HGRN_W, nc), lambda i, j: (layer, 0, j)),
            pl.BlockSpec((None, RWKV_W, nc), lambda i, j: (layer, 0, j)),
            pl.BlockSpec((None, nc, D_MODEL), lambda i, j: (layer, j, 0)),
            _layer_spec(gpost, layer),
        ],
        out_specs=pl.BlockSpec((tm, D_MODEL), row),
        out_shape=jax.ShapeDtypeStruct((rows, D_MODEL), F32),
        scratch_shapes=[pltpu.VMEM((tm, D_MODEL), BF16), pltpu.VMEM((tm, D_MODEL), F32)],
        compiler_params=pltpu.CompilerParams(
            dimension_semantics=("parallel", "arbitrary"), vmem_limit_bytes=VMEM_LIMIT),
        name=name,
    )(x, gpre, ya, yb, yc, wg, wpa, wpb, wpc, wo, gpost)


def _lane_mean(x, j_b):
    n = x.shape[0]
    hi = _bf(x)
    mid = _bf(x - hi.astype(F32))
    out = _dot(jnp.concatenate([hi, mid], axis=0), j_b)
    return out[0:n] + out[n:]


def _mlstm_stages(pa_ref, pc_ref, bias_i_ref, bias_f_ref, norm_ref, y_ref, cn_scr, m_scr, stores,
                  *, length):
    dh = MLSTM_DH
    heads = range(MLSTM_H)
    sls = [slice(h * dh, (h + 1) * dh) for h in heads]
    col = lambda part, h: slice(part * MLSTM_W + h * dh, part * MLSTM_W + (h + 1) * dh)
    cns = [cn_scr[h] for h in heads]
    m_prev = m_scr[...]

    causal = _tri(length)
    tri_b = _bf(jnp.where(causal, 1.0, 0.0))
    gate0 = IF_BLOCK * LANES
    ig = pc_ref[:, gate0:gate0 + LANES] + bias_i_ref[...]
    fg = pc_ref[:, gate0 + LANES:gate0 + 2 * LANES] + bias_f_ref[...]
    log_f = jnp.minimum(fg, 0.0) - jnp.log1p(jnp.exp(-jnp.abs(fg)))
    b_all = _cumsum_time(log_f, tri_b)
    yield
    c_all = ig - b_all
    row = lax.broadcasted_iota(jnp.int32, (length, LANES), 0)
    run_max = c_all
    shift = 1
    while shift < length:
        run_max = jnp.maximum(
            run_max, jnp.where(row >= shift, pltpu.roll(run_max, shift, 0), -jnp.inf))
        shift *= 2
    mx_all = jnp.maximum(run_max, m_prev)
    mt_all = b_all + mx_all
    mx_end = mx_all[length - 1:length]
    m_new = b_all[length - 1:length] + mx_end
    w_old_all = jnp.exp(m_prev - mx_end)
    c_t = c_all.T
    ones = jnp.ones((length, dh), F32)
    j_b = jnp.full((dh, dh), 1.0 / dh, BF16)
    scale = MLSTM_DH ** -0.5
    yield
    k_ts = [(pa_ref[:, col(1, h)] * scale).T for h in heads]
    qbs = [_bf(pa_ref[:, col(0, h)]) for h in heads]
    s_raw = [_dot(qbs[h], _bf(k_ts[h])) for h in heads]
    yield
    inter = [_dot(qbs[h], _bf(cns[h])) for h in heads]
    yield
    mx_bs = [jnp.broadcast_to(mx_all[:, h:h + 1], (length, dh)) for h in heads]
    es = [jnp.exp(jnp.where(causal, c_t[h:h + 1, :] - mx_bs[h][:, 0:length], -jnp.inf))
          for h in heads]
    yield
    v1s = [_bf(jnp.concatenate([pa_ref[:, col(2, h)], ones], axis=1)) for h in heads]
    tots = []
    for h in heads:
        w_inter = jnp.exp(m_prev[:, h:h + 1] - mx_bs[h])
        tots.append(jnp.concatenate([w_inter, w_inter], axis=1) * inter[h]
                    + _dot(_bf(s_raw[h] * es[h]), v1s[h]))
    yield
    new_cn = [w_old_all[:, h:h + 1] * cns[h]
              + _dot(_bf(k_ts[h] * es[h][length - 1:length, :]), v1s[h]) for h in heads]
    yield
    cens = [tots[h][:, 0:dh] - _lane_mean(tots[h][:, 0:dh], j_b) for h in heads]
    yield
    vars_ = [_lane_mean(cens[h] * cens[h], j_b) for h in heads]
    yield
    outs = []
    for h in heads:
        mt_b = jnp.broadcast_to(mt_all[:, h:h + 1], (length, dh))
        inv = 1.0 / jnp.maximum(jnp.abs(tots[h][:, dh:]), jnp.exp(-mt_b))
        hid = cens[h] * inv * lax.rsqrt(vars_[h] * inv * inv + EPS) * norm_ref[:, sls[h]]
        outs.append(_bf(hid * _sigmoid(pa_ref[:, col(3, h)]) * _silu(pa_ref[:, col(4, h)])))

    def store():
        y_ref[...] = jnp.concatenate(outs, axis=1)
        for h in heads:
            cn_scr[h] = new_cn[h]
        m_scr[...] = m_new

    stores.append(store)


def _hgrn_stages(pb_ref, lb_ref, norm_ref, y_ref, st_scr, stores, *, length):
    dk = HGRN_DK
    sub = min(HGRN_SUB, length)
    n_sub = length // sub
    heads = range(HGRN_H)
    sls = [slice(h * dk, (h + 1) * dk) for h in heads]
    col = lambda part, h: slice(part * HGRN_W + h * dk, part * HGRN_W + (h + 1) * dk)
    sts = [st_scr[h] for h in heads]

    lb = lb_ref[...]
    fp = pb_ref[:, HGRN_W:2 * HGRN_W]
    f_gate = lb + (1.0 - lb) * _sigmoid(fp)
    k_all = (1.0 - lb) * _sigmoid(-fp)
    tri_b = _bf(jnp.where(_tri(length), 1.0, 0.0))
    a_all = _cumsum_time(jnp.log(f_gate), tri_b)
    lane_l = lax.broadcasted_iota(jnp.int32, (sub, length), 1)
    row_l = lax.broadcasted_iota(jnp.int32, (sub, length), 0)
    yield
    qs = [pb_ref[:, col(0, h)] for h in heads]
    ks = [k_all[:, sl] for sl in sls]
    avs = [a_all[:, sl] for sl in sls]
    a2_all = a_all * LOG2E
    a2s = [a2_all[:, sl] for sl in sls]
    vbs = [_bf(pb_ref[:, col(2, h)]) for h in heads]
    o_inter = [_dot_nt(_bf(qs[h] * jnp.exp(avs[h])), _bf(sts[h])) for h in heads]
    yield
    new_st = []
    for h in heads:
        a_end = avs[h][length - 1:length]
        new_st.append(jnp.exp(a_end) * sts[h]
                      + _dot_tn(vbs[h], _bf(ks[h] * jnp.exp(a_end - avs[h]))))
    yield

    blocks = [[] for _ in heads]
    for i in range(n_sub):
        r0 = i * sub
        blks = []
        for h in heads:
            if i > 0:
                ref_row = avs[h][r0:r0 + 1]
                q_s = _bf(qs[h][r0:r0 + sub] * jnp.exp(avs[h][r0:r0 + sub] - ref_row))
                k_s = jnp.concatenate(
                    [ks[h][0:r0] * jnp.exp(ref_row - avs[h][0:r0]),
                     jnp.zeros((length - r0, dk), F32)], axis=0)
                blks.append(_dot_nt(q_s, _bf(k_s)))
            else:
                blks.append(jnp.zeros((sub, length), F32))
        for s_idx in range(sub):
            for h in heads:
                a_i = a2s[h][r0:r0 + sub]
                e = jnp.exp2(jnp.minimum(a_i - a_i[s_idx:s_idx + 1], 0.0))
                col_v = jnp.sum(qs[h][r0:r0 + sub] * e * ks[h][r0 + s_idx:r0 + s_idx + 1],
                                axis=-1, keepdims=True)
                blks[h] = jnp.where(lane_l == r0 + s_idx, col_v, blks[h])
        yield
        for h in heads:
            blocks[h].append(jnp.where(lane_l <= r0 + row_l, blks[h], 0.0))

    outs = []
    for h in heads:
        scores = jnp.concatenate(blocks[h], axis=0) if n_sub > 1 else blocks[h][0]
        o = o_inter[h] + _dot(_bf(scores), vbs[h])
        o = o * lax.rsqrt(jnp.mean(o * o, axis=-1, keepdims=True) + EPS)
        outs.append(_bf(o * norm_ref[:, sls[h]] * _silu(pb_ref[:, col(3, h)])))

    def store():
        y_ref[...] = jnp.concatenate(outs, axis=1)
        for h in heads:
            st_scr[h] = new_st[h]

    stores.append(store)


def _pair_sum(x, even):
    s_even = jnp.sum(jnp.where(even, x, 0.0), axis=-1, keepdims=True)
    s_odd = jnp.sum(jnp.where(even, 0.0, x), axis=-1, keepdims=True)
    return jnp.where(even, s_even, s_odd)


def _rwkv_stages(pc_ref, mu_ref, wa_up_ref, w0_ref, a0_ref, kk_ref, ka_ref, rk_ref, gg_ref, gb_ref,
                 y_ref, bd_scr, carry_scr, stores, *, length):
    dh = RWKV_DH
    w = RWKV_W
    pairs = range(RWKV_H // 2)
    sls = [slice(p * LANES, (p + 1) * LANES) for p in pairs]
    l2 = 2 * length
    bds = [bd_scr[p] for p in pairs]
    gn_g = gg_ref[...]
    gn_b = gb_ref[...]

    pc = pc_ref[:, 0:C_COLS]
    row0 = lax.broadcasted_iota(jnp.int32, (length, C_COLS), 0) == 0
    prev = jnp.where(row0, carry_scr[...], pltpu.roll(pc, 1, 0))
    xs = pc + mu_ref[...] * (prev - pc)
    r = xs[:, 0:w]
    k = xs[:, w:2 * w]
    v = xs[:, 2 * w:3 * w]
    low = xs[:, 3 * w:3 * w + 2 * RWKV_RANK]
    z = xs[:, 3 * w + 2 * RWKV_RANK:]
    lane = lax.broadcasted_iota(jnp.int32, (length, 2 * RWKV_RANK), 1)
    low = jnp.where(lane < RWKV_RANK, jnp.tanh(low), low)
    up = _dot(_bf(low), wa_up_ref[...])
    yield
    w_logit = -_softplus(-(w0_ref[...] + up[:, 0:w])) - 0.5
    log_w = -jnp.exp(w_logit)
    a = _sigmoid(a0_ref[...] + up[:, w:2 * w])
    kk_raw = k * kk_ref[...]
    k2 = k * (1.0 + (a - 1.0) * ka_ref[...])
    rk_bonus = r * k2 * rk_ref[...]
    tri_b = _bf(jnp.where(_tri(length), 1.0, 0.0))
    cum = _cumsum_time(log_w, tri_b)
    yield
    p_in = jnp.exp(cum)
    p_ex = jnp.exp(cum - log_w)
    p_inv = jnp.exp(-cum)
    cum_end = cum[length - 1:length]
    p_end = jnp.exp(cum_end - cum)
    dec_end = jnp.exp(cum_end)

    even = lax.broadcasted_iota(jnp.int32, (length, LANES), 1) < dh
    rows2 = lax.broadcasted_iota(jnp.int32, (l2, l2), 0)
    cols2 = lax.broadcasted_iota(jnp.int32, (l2, l2), 1)
    same = (rows2 >= length) == (cols2 >= length)
    strict_bd = same & (rows2 > cols2)
    incl_bd = same & (rows2 >= cols2)
    rows_s = lax.broadcasted_iota(jnp.int32, (LANES, LANES), 0)
    cols_s = lax.broadcasted_iota(jnp.int32, (LANES, LANES), 1)
    state_bd = (rows_s >= dh) == (cols_s >= dh)
    n_double = int(math.log2(length))
    assert 2 ** n_double == length

    def stack(x):
        return jnp.concatenate([jnp.where(even, x, 0.0), jnp.where(even, 0.0, x)], axis=0)

    kas, lhss, grams = [], [], []
    for p in pairs:
        sl = sls[p]
        kk_r = kk_raw[:, sl]
        kk = kk_r / jnp.maximum(jnp.sqrt(_pair_sum(kk_r * kk_r, even)), 1e-12)
        ka = kk * a[:, sl]
        lhs = _bf(jnp.concatenate([stack(-kk * p_ex[:, sl]), stack(r[:, sl] * p_in[:, sl])], axis=0))
        bh = _bf(ka * p_inv[:, sl])
        kh = _bf(k2[:, sl] * p_inv[:, sl])
        rhs = jnp.concatenate([bh, bh, kh, kh], axis=0)
        grams.append(_dot_nt(lhs, rhs))
        kas.append(ka)
        lhss.append(lhs)
    yield
    uy0s = [_dot_nt(lhss[p], _bf(bds[p])) for p in pairs]
    v_stacks = [_bf(stack(v[:, sl])) for sl in sls]
    yield
    n_mats = [jnp.where(strict_bd, grams[p][0:l2, 0:l2], 0.0) for p in pairs]
    xs_u = [uy0s[p][0:l2] + _dot(_bf(jnp.where(strict_bd, grams[p][0:l2, l2:], 0.0)), v_stacks[p])
            for p in pairs]
    abks = [_bf(jnp.concatenate([jnp.where(incl_bd, grams[p][l2:, 0:l2], 0.0),
                                 jnp.where(incl_bd, grams[p][l2:, l2:], 0.0)], axis=1))
            for p in pairs]
    yield
    for step in range(n_double):
        for p in pairs:
            n_b = _bf(n_mats[p])
            xs_u[p] = xs_u[p] + _dot(n_b, _bf(xs_u[p]))
            if step + 1 < n_double:
                n_mats[p] = _dot(n_b, n_b)
        yield

    new_bd, ys = [], []
    for p in pairs:
        sl = sls[p]
        u_st = xs_u[p]
        y_st = uy0s[p][l2:] + _dot(abks[p], jnp.concatenate([_bf(u_st), v_stacks[p]], axis=0))
        ys.append(y_st[0:length] + y_st[length:])
        u_pair = u_st[0:length] + u_st[length:]
        upd = _dot_tn(_bf(jnp.concatenate([u_pair, v[:, sl]], axis=0)),
                      _bf(jnp.concatenate([kas[p] * p_end[:, sl], k2[:, sl] * p_end[:, sl]], axis=0)))
        new_bd.append(bds[p] * dec_end[:, sl] + jnp.where(state_bd, upd, 0.0))
    yield
    outs = []
    for p in pairs:
        sl = sls[p]
        mu = _pair_sum(ys[p], even) * (1.0 / dh)
        cen = ys[p] - mu
        var = _pair_sum(cen * cen, even) * (1.0 / dh)
        y = cen * lax.rsqrt(var + RWKV_GN_EPS) * gn_g[:, sl] + gn_b[:, sl]
        bonus = _pair_sum(rk_bonus[:, sl], even) * v[:, sl]
        outs.append(_bf((y + bonus) * _silu(z[:, sl])))

    def store():
        y_ref[...] = jnp.concatenate(outs, axis=1)
        carry_scr[...] = pc[length - 1:length, :]
        for p in pairs:
            bd_scr[p] = new_bd[p]

    stores.append(store)


def _branch_kernel(pa_ref, pb_ref, pc_ref, bias_i_ref, bias_f_ref, anorm_ref, lb_ref, hnorm_ref,
                   mu_ref, wa_up_ref, w0_ref, a0_ref, kk_ref, ka_ref, rk_ref, gg_ref, gb_ref,
                   c0_ref, n0_ref, m0_ref, hs0_ref, rs0_ref, shift0_ref,
                   ya_ref, yb_ref, yc_ref, c_out, n_out, m_out, hs_out, rs_out, shift_out,
                   cn_scr, m_scr, st_scr, bd_scr, carry_scr, *, length, n_grp):
    c_idx = pl.program_id(1)
    dh = MLSTM_DH
    rh = RWKV_DH
    streams = range(n_grp)

    def rows(ref, g):
        return ref.at[g] if len(ref.shape) == 3 else ref.at[pl.ds(g * length, length)]

    @pl.when(c_idx == 0)
    def _():
        for g in streams:
            n_t = n0_ref[g].T
            for h in range(MLSTM_H):
                cn_scr[g, h, :, 0:dh] = c0_ref[g, h].T
                cn_scr[g, h, :, dh:2 * dh] = jnp.broadcast_to(n_t[:, h:h + 1], (dh, dh))
            m_scr[g] = m0_ref[g]
            for h in range(HGRN_H):
                st_scr[g, h] = hs0_ref[g, h].T
            zero = jnp.zeros((rh, rh), F32)
            for p in range(RWKV_H // 2):
                top = jnp.concatenate([rs0_ref[g, 2 * p], zero], axis=1)
                bot = jnp.concatenate([zero, rs0_ref[g, 2 * p + 1]], axis=1)
                bd_scr[g, p] = jnp.concatenate([top, bot], axis=0)
            carry_scr[g] = shift0_ref[g]

    stores = []
    active = (
        [_rwkv_stages(rows(pc_ref, g), mu_ref, wa_up_ref, w0_ref, a0_ref, kk_ref, ka_ref, rk_ref,
                      gg_ref, gb_ref, yc_ref.at[g], bd_scr.at[g], carry_scr.at[g], stores,
                      length=length) for g in streams]
        + [_mlstm_stages(rows(pa_ref, g), rows(pc_ref, g), bias_i_ref, bias_f_ref, anorm_ref,
                         ya_ref.at[g], cn_scr.at[g], m_scr.at[g], stores, length=length)
           for g in streams]
        + [_hgrn_stages(rows(pb_ref, g), lb_ref, hnorm_ref, yb_ref.at[g], st_scr.at[g], stores,
                        length=length) for g in streams])
    while active:
        for gen in list(active):
            if next(gen, StopIteration) is StopIteration:
                active.remove(gen)
    for store in stores:
        store()

    @pl.when(c_idx == pl.num_programs(1) - 1)
    def _():
        for g in streams:
            for h in range(MLSTM_H):
                c_out[g, h] = cn_scr[g, h, :, 0:dh].T
                n_out[g, h:h + 1, :] = cn_scr[g, h, :, dh:2 * dh].T[0:1, :]
            m_out[g] = m_scr[g]
            for h in range(HGRN_H):
                hs_out[g, h] = st_scr[g, h].T
            for p in range(RWKV_H // 2):
                rs_out[g, 2 * p] = bd_scr[g, p, 0:rh, 0:rh]
                rs_out[g, 2 * p + 1] = bd_scr[g, p, rh:2 * rh, rh:2 * rh]
            shift_out[g] = carry_scr[g]


def _branch_call(pa, pb, pc, p, states, layer, state_layer, *, n_seq, n_chunks, length, row_off,
                 name):
    grp = BRANCH_GROUP
    assert n_seq % grp == 0
    if n_chunks == 1:
        assert row_off % (grp * length) == 0
        blk0 = row_off // (grp * length)
        views = (pa, pb, pc)
        row_spec = lambda width: pl.BlockSpec((grp * length, width), lambda b, c: (blk0 + b, 0))
    else:
        assert row_off == 0 and pa.shape[0] == n_seq * n_chunks * length
        views = tuple(a.reshape(n_seq, n_chunks * length, a.shape[1]) for a in (pa, pb, pc))
        row_spec = lambda width: pl.BlockSpec((grp, length, width), lambda b, c: (b, c, 0))
    out_spec = lambda width: pl.BlockSpec((grp, length, width), lambda b, c: (b, c, 0))
    state_tails = [(MLSTM_H, MLSTM_DH, MLSTM_DH), (MLSTM_H, MLSTM_DH), (1, LANES),
                   (HGRN_H, HGRN_DK, HGRN_DV), (RWKV_H, RWKV_DH, RWKV_DH), (1, C_COLS)]
    state_in = [pl.BlockSpec((None, grp) + t, lambda b, c, t=t: (state_layer, b) + (0,) * len(t))
                for t in state_tails]
    state_out = [pl.BlockSpec((grp,) + t, lambda b, c, t=t: (b,) + (0,) * len(t))
                 for t in state_tails]
    state_shapes = [jax.ShapeDtypeStruct((n_seq,) + t, F32) for t in state_tails]
    params = [p['bias_i'], p['bias_f'], p['mlstm_norm'], p['lb'], p['hgrn_norm'], p['mu'],
              p['wa_up'], p['w0'], p['a0'], p['k_k'], p['k_a'], p['r_k'], p['gn_g'], p['gn_b']]
    t_len = n_chunks * length
    outs = pl.pallas_call(
        functools.partial(_branch_kernel, length=length, n_grp=grp),
        grid=(n_seq // grp, n_chunks),
        in_specs=[row_spec(A_W), row_spec(B_W), row_spec(C_W)]
        + [_layer_spec(a, layer) for a in params] + state_in,
        out_specs=[out_spec(MLSTM_W), out_spec(HGRN_W), out_spec(RWKV_W)] + state_out,
        out_shape=[jax.ShapeDtypeStruct((n_seq, t_len, MLSTM_W), BF16),
                   jax.ShapeDtypeStruct((n_seq, t_len, HGRN_W), BF16),
                   jax.ShapeDtypeStruct((n_seq, t_len, RWKV_W), BF16)] + state_shapes,
        scratch_shapes=[pltpu.VMEM((grp, MLSTM_H, MLSTM_DH, 2 * MLSTM_DH), F32),
                        pltpu.VMEM((grp, 1, LANES), F32),
                        pltpu.VMEM((grp, HGRN_H, HGRN_DV, HGRN_DK), F32),
                        pltpu.VMEM((grp, RWKV_H // 2, 2 * RWKV_DH, 2 * RWKV_DH), F32),
                        pltpu.VMEM((grp, 1, C_COLS), F32)],
        compiler_params=pltpu.CompilerParams(
            dimension_semantics=("parallel", "arbitrary"), vmem_limit_bytes=VMEM_LIMIT),
        name=name,
    )(*views, *params, *states)
    ys = tuple(y.reshape(n_seq * t_len, y.shape[2]) for y in outs[0:3])
    return ys, outs[3:]


def _layer_weights(w_t):
    a_cols = 5 * MLSTM_W + 2 * MLSTM_H
    if0 = 3 * MLSTM_W
    b0 = a_cols
    c0 = b0 + B_W
    g0 = c0 + C_COLS
    w_a = _bf(jnp.concatenate([w_t[0:if0], w_t[if0 + 2 * MLSTM_H:a_cols]], axis=0))
    w_b = _bf(w_t[b0:c0])
    gate_pad = jnp.zeros((LANES - MLSTM_H, D_MODEL), F32)
    w_c = _bf(jnp.concatenate(
        [w_t[c0:g0], w_t[if0:if0 + MLSTM_H], gate_pad,
         w_t[if0 + MLSTM_H:if0 + 2 * MLSTM_H], gate_pad,
         jnp.zeros((LANES, D_MODEL), F32)], axis=0))
    n_chunk = D_MODEL // MERGE_NC
    w_g = _bf(jnp.concatenate(
        [w_t[g0 + b * D_MODEL + j * MERGE_NC:g0 + b * D_MODEL + (j + 1) * MERGE_NC]
         for j in range(n_chunk) for b in range(3)], axis=0))
    return w_a, w_b, w_c, w_g


def _lane_pad(m):
    pad = [(0, 0)] * (m.ndim - 1) + [(0, LANES - m.shape[-1])]
    return jnp.expand_dims(jnp.pad(m.astype(F32), pad), -2)


def _stacked_params(lb_all, norm_pre, norm_post, mlstm_b_i, mlstm_b_f, mlstm_norm, hgrn_norm,
                    rwkv_mu, rwkv_w0, rwkv_w_up, rwkv_a0, rwkv_a_up, rwkv_k_k, rwkv_k_a, rwkv_r_k,
                    rwkv_gn_g, rwkv_gn_b, w_proj_a, w_proj_b, w_proj_c, w_out):
    row = lambda a: a.astype(F32)[:, None, :]
    zero = jnp.zeros(rwkv_w_up.shape, F32)
    wa_up = _bf(jnp.concatenate([jnp.concatenate([rwkv_w_up, zero], axis=2),
                                 jnp.concatenate([zero, rwkv_a_up], axis=2)], axis=1))
    return dict(
        norm_pre=row(norm_pre), norm_post=row(norm_post), bias_i=_lane_pad(mlstm_b_i),
        bias_f=_lane_pad(mlstm_b_f), mlstm_norm=row(mlstm_norm), lb=row(lb_all),
        hgrn_norm=row(hgrn_norm), mu=row(rwkv_mu), wa_up=wa_up, w0=row(rwkv_w0), a0=row(rwkv_a0),
        k_k=row(rwkv_k_k), k_a=row(rwkv_k_a), r_k=row(rwkv_r_k), gn_g=row(rwkv_gn_g),
        gn_b=row(rwkv_gn_b), wpa=_bf(w_proj_a), wpb=_bf(w_proj_b), wpc=_bf(w_proj_c),
        wo=_bf(w_out))


def _branches(p, layer, proj_main, proj_tail, st_s, *, bp, t_p, bs, t_s):
    tails = [(MLSTM_H, MLSTM_DH, MLSTM_DH), (MLSTM_H, MLSTM_DH), (1, LANES),
             (HGRN_H, HGRN_DK, HGRN_DV), (RWKV_H, RWKV_DH, RWKV_DH), (1, C_COLS)]
    st_zero = tuple(jnp.zeros((1, bp) + t, F32) for t in tails)
    y_meta, st_meta = _branch_call(*proj_tail, p, st_zero, layer, 0, n_seq=bp, n_chunks=1,
                                   length=N_META, row_off=bs * t_s, name=f'branch_meta{layer}')
    y_main, st_p = _branch_call(*proj_main, p, tuple(s[None] for s in st_meta), layer, 0,
                                n_seq=bp, n_chunks=t_p // CHUNK, length=CHUNK, row_off=0,
                                name=f'branch_main{layer}')
    y_samp, st_so = _branch_call(*proj_tail, p, st_s, layer, layer, n_seq=bs, n_chunks=1,
                                 length=t_s, row_off=0, name=f'branch_samp{layer}')
    y_tail = tuple(jnp.concatenate([a, b], axis=0) for a, b in zip(y_samp, y_meta))
    unpad = lambda st: (st[0], st[1], st[2][:, 0, 0:MLSTM_H]) + tuple(st[3:])
    return y_main, y_tail, unpad(st_p), unpad(st_so)


def kernel(x_prompt, x_sample, state_mlstm_C, state_mlstm_n, state_mlstm_m, state_hgrn_S,
           state_rwkv_S, cache_rwkv_shift, meta_tokens, norm_pre, norm_post, w_in,
           mlstm_b_i, mlstm_b_f, mlstm_norm, hgrn_lb_logits, hgrn_norm, rwkv_mu, rwkv_w0,
           rwkv_w_up, rwkv_a0, rwkv_a_up, rwkv_k_k, rwkv_k_a, rwkv_r_k, rwkv_gn_g, rwkv_gn_b,
           w_proj_a, w_proj_b, w_proj_c, w_out):
    bp, t_p, _ = x_prompt.shape
    bs, t_s, _ = x_sample.shape
    depth = w_in.shape[0]
    assert t_p % CHUNK == 0 and t_s % HGRN_SUB == 0 and (bs * t_s) % (BRANCH_GROUP * N_META) == 0

    sm = jax.nn.softmax(hgrn_lb_logits.astype(F32), axis=0)
    lb_all = jnp.cumsum(sm, axis=0) - sm[0]

    x_main = x_prompt.reshape(bp * t_p, D_MODEL)
    meta = jnp.broadcast_to(meta_tokens.astype(F32)[None], (bp, N_META, D_MODEL))
    x_tail = jnp.concatenate([x_sample.reshape(bs * t_s, D_MODEL),
                              meta.reshape(bp * N_META, D_MODEL)], axis=0)

    p = _stacked_params(lb_all, norm_pre, norm_post, mlstm_b_i, mlstm_b_f, mlstm_norm, hgrn_norm,
                        rwkv_mu, rwkv_w0, rwkv_w_up, rwkv_a0, rwkv_a_up, rwkv_k_k, rwkv_k_a,
                        rwkv_r_k, rwkv_gn_g, rwkv_gn_b, w_proj_a, w_proj_b, w_proj_c, w_out)
    st_s = (state_mlstm_C.astype(F32), state_mlstm_n.astype(F32), _lane_pad(state_mlstm_m),
            state_hgrn_S.astype(F32), state_rwkv_S.astype(F32), cache_rwkv_shift.astype(F32))

    w_t = jnp.swapaxes(w_in, 1, 2)

    outs_p, outs_s = [], []
    for l in range(depth):
        weights = _layer_weights(w_t[l])
        names = ('w_a', 'w_b', 'w_c')
        proj_main = tuple(_proj(x_main, p['norm_pre'], l, w, f'proj_{k}_main{l}')
                          for k, w in zip(names, weights[0:3]))
        proj_tail = tuple(_proj(x_tail, p['norm_pre'], l, w, f'proj_{k}_tail{l}')
                          for k, w in zip(names, weights[0:3]))
        y_main, y_tail, st_p_out, st_s_out = _branches(
            p, l, proj_main, proj_tail, st_s, bp=bp, t_p=t_p, bs=bs, t_s=t_s)
        merge_args = (weights[3], p['wpa'], p['wpb'], p['wpc'], p['wo'], p['norm_post'], l)
        x_main = _merge(x_main, p['norm_pre'], *y_main, *merge_args, f'merge_main{l}')
        x_tail = _merge(x_tail, p['norm_pre'], *y_tail, *merge_args, f'merge_tail{l}')
        outs_p.append(st_p_out)
        outs_s.append(st_s_out)

    states_p = tuple(jnp.stack([o[j] for o in outs_p]) for j in range(6))
    states_s = tuple(jnp.stack([o[j] for o in outs_s]) for j in range(6))
    y_prompt = x_main.reshape(bp, t_p, D_MODEL)
    y_sample = x_tail[0:bs * t_s].reshape(bs, t_s, D_MODEL)
    return (y_prompt, y_sample) + states_p + states_s
```

```python
import functools
import math

import jax
import jax.numpy as jnp
from jax import lax
from jax.experimental import pallas as pl
from jax.experimental.pallas import tpu as pltpu

F32 = jnp.float32
BF16 = jnp.bfloat16

D_MODEL = 2048
CHUNK = 64
N_META = 16
EPS = 1e-6

MLSTM_H = 8
MLSTM_DH = 128
MLSTM_W = MLSTM_H * MLSTM_DH
HGRN_H = 4
HGRN_DK = 128
HGRN_DV = 128
HGRN_W = HGRN_H * HGRN_DV
RWKV_H = 8
RWKV_DH = 64
RWKV_W = RWKV_H * RWKV_DH
RWKV_RANK = 64
RWKV_GN_EPS = 64e-5
C_COLS = 4 * RWKV_W + 2 * RWKV_RANK

LANES = 128
HGRN_SUB = 8
LOG2E = 1.4426950408889634
A_W = 5 * MLSTM_W
B_W = 4 * HGRN_W
C_W = C_COLS + 3 * LANES
IF_BLOCK = C_COLS // LANES
MERGE_NC = 512
BRANCH_GROUP = 2
VMEM_LIMIT = 56 * 1024 * 1024


def _dot(a, b):
    return jnp.dot(a, b, preferred_element_type=F32)


def _dot_nt(a, b):
    return lax.dot_general(a, b, (((1,), (1,)), ((), ())), preferred_element_type=F32)


def _dot_tn(a, b):
    return lax.dot_general(a, b, (((0,), (0,)), ((), ())), preferred_element_type=F32)


def _bf(a):
    return a.astype(BF16)


def _sigmoid(x):
    return 1.0 / (1.0 + jnp.exp(-x))


def _silu(x):
    return x * _sigmoid(x)


def _softplus(x):
    return jnp.maximum(x, 0.0) + jnp.log1p(jnp.exp(-jnp.abs(x)))


def _tri(length, strict=False):
    row = lax.broadcasted_iota(jnp.int32, (length, length), 0)
    col = lax.broadcasted_iota(jnp.int32, (length, length), 1)
    return (row > col) if strict else (row >= col)


def _cumsum_time(x, tri_b):
    hi = _bf(x)
    r1 = x - hi.astype(F32)
    mid = _bf(r1)
    lo = _bf(r1 - mid.astype(F32))
    return _dot(tri_b, hi) + _dot(tri_b, mid) + _dot(tri_b, lo)


def _rmsnorm(x, g):
    return x * lax.rsqrt(jnp.mean(x * x, axis=-1, keepdims=True) + EPS) * g


def _row_tile(rows, cap):
    best = None
    for t in range(16, min(rows, cap) + 1, 16):
        if rows % t == 0:
            best = t
    assert best is not None, rows
    return best


def _col_tile(cols, cap):
    best = None
    for t in range(LANES, min(cols, cap) + 1, LANES):
        if cols % t == 0:
            best = t
    assert best is not None, cols
    return best


def _proj_kernel(x_ref, g_ref, w_ref, o_ref, *h_scr):
    if h_scr:
        @pl.when(pl.program_id(1) == 0)
        def _():
            h_scr[0][...] = _bf(_rmsnorm(x_ref[...], g_ref[...]))

        o_ref[...] = _dot(h_scr[0][...], w_ref[...])
    else:
        o_ref[...] = _dot(_bf(_rmsnorm(x_ref[...], g_ref[...])), w_ref[...])


def _layer_spec(arr, layer):
    tail = arr.shape[1:]
    return pl.BlockSpec((None,) + tail, lambda *_: (layer,) + (0,) * len(tail))


def _proj(x, g, layer, w, name):
    rows, _ = x.shape
    cols = w.shape[1]
    tm = _row_tile(rows, 1024)
    tn = _col_tile(cols, 1280)
    if cols // tn >= 4:
        grid, at = (rows // tm, cols // tn), (lambda i, j: (i, j))
        scratch, semantics = [pltpu.VMEM((tm, D_MODEL), BF16)], ("parallel", "arbitrary")
    else:
        grid, at = (cols // tn, rows // tm), (lambda j, i: (i, j))
        scratch, semantics = [], ("parallel", "parallel")
    return pl.pallas_call(
        _proj_kernel,
        grid=grid,
        in_specs=[
            pl.BlockSpec((tm, D_MODEL), lambda *ids: (at(*ids)[0], 0)),
            _layer_spec(g, layer),
            pl.BlockSpec((D_MODEL, tn), lambda *ids: (0, at(*ids)[1])),
        ],
        out_specs=pl.BlockSpec((tm, tn), at),
        out_shape=jax.ShapeDtypeStruct((rows, cols), F32),
        scratch_shapes=scratch,
        compiler_params=pltpu.CompilerParams(
            dimension_semantics=semantics, vmem_limit_bytes=VMEM_LIMIT),
        name=name,
    )(x, g, w)


def _merge_kernel(x_ref, gpre_ref, ya_ref, yb_ref, yc_ref, wg_ref,
                  wpa_ref, wpb_ref, wpc_ref, wo_ref, gpost_ref, o_ref, h_scr, acc_scr):
    j = pl.program_id(1)
    nc = MERGE_NC

    @pl.when(j == 0)
    def _():
        h_scr[...] = _bf(_rmsnorm(x_ref[...], gpre_ref[...]))
        acc_scr[...] = jnp.zeros_like(acc_scr)

    gates = _sigmoid(_dot(h_scr[...], wg_ref[...]))
    merged = (gates[:, 0:nc] * _dot(ya_ref[...], wpa_ref[...])
              + gates[:, nc:2 * nc] * _dot(yb_ref[...], wpb_ref[...])
              + gates[:, 2 * nc:3 * nc] * _dot(yc_ref[...], wpc_ref[...]))
    acc_scr[...] += _dot(_bf(merged), wo_ref[...])

    @pl.when(j == pl.num_programs(1) - 1)
    def _():
        o_ref[...] = x_ref[...] + _rmsnorm(acc_scr[...], gpost_ref[...])


def _merge(x, gpre, ya, yb, yc, wg, wpa, wpb, wpc, wo, gpost, layer, name):
    rows = x.shape[0]
    tm = _row_tile(rows, 640)
    nc = MERGE_NC
    n_chunk = D_MODEL // nc
    row = lambda i, j: (i, 0)
    return pl.pallas_call(
        _merge_kernel,
        grid=(rows // tm, n_chunk),
        in_specs=[
            pl.BlockSpec((tm, D_MODEL), row),
            _layer_spec(gpre, layer),
            pl.BlockSpec((tm, MLSTM_W), row),
            pl.BlockSpec((tm, HGRN_W), row),
            pl.BlockSpec((tm, RWKV_W), row),
            pl.BlockSpec((D_MODEL, 3 * nc), lambda i, j: (0, j)),
            pl.BlockSpec((None, MLSTM_W, nc), lambda i, j: (layer, 0, j)),
            pl.BlockSpec((None, HGRN_W, nc), lambda i, j: (layer, 0, j)),
            pl.BlockSpec((None, RWKV_W, nc), lambda i, j: (layer, 0, j)),
            pl.BlockSpec((None, nc, D_MODEL), lambda i, j: (layer, j, 0)),
            _layer_spec(gpost, layer),
        ],
        out_specs=pl.BlockSpec((tm, D_MODEL), row),
        out_shape=jax.ShapeDtypeStruct((rows, D_MODEL), F32),
        scratch_shapes=[pltpu.VMEM((tm, D_MODEL), BF16), pltpu.VMEM((tm, D_MODEL), F32)],
        compiler_params=pltpu.CompilerParams(
            dimension_semantics=("parallel", "arbitrary"), vmem_limit_bytes=VMEM_LIMIT),
        name=name,
    )(x, gpre, ya, yb, yc, wg, wpa, wpb, wpc, wo, gpost)


def _lane_mean(x, j_b):
    n = x.shape[0]
    hi = _bf(x)
    mid = _bf(x - hi.astype(F32))
    out = _dot(jnp.concatenate([hi, mid], axis=0), j_b)
    return out[0:n] + out[n:]


def _mlstm_stages(pa_ref, pc_ref, bias_i_ref, bias_f_ref, norm_ref, y_ref, cn_scr, m_scr, stores,
                  *, length):
    dh = MLSTM_DH
    heads = range(MLSTM_H)
    sls = [slice(h * dh, (h + 1) * dh) for h in heads]
    col = lambda part, h: slice(part * MLSTM_W + h * dh, part * MLSTM_W + (h + 1) * dh)
    cns = [cn_scr[h] for h in heads]
    m_prev = m_scr[...]

    causal = _tri(length)
    tri_b = _bf(jnp.where(causal, 1.0, 0.0))
    gate0 = IF_BLOCK * LANES
    ig = pc_ref[:, gate0:gate0 + LANES] + bias_i_ref[...]
    fg = pc_ref[:, gate0 + LANES:gate0 + 2 * LANES] + bias_f_ref[...]
    log_f = jnp.minimum(fg, 0.0) - jnp.log1p(jnp.exp(-jnp.abs(fg)))
    b_all = _cumsum_time(log_f, tri_b)
    yield
    c_all = ig - b_all
    row = lax.broadcasted_iota(jnp.int32, (length, LANES), 0)
    run_max = c_all
    shift = 1
    while shift < length:
        run_max = jnp.maximum(
            run_max, jnp.where(row >= shift, pltpu.roll(run_max, shift, 0), -jnp.inf))
        shift *= 2
    mx_all = jnp.maximum(run_max, m_prev)
    mt_all = b_all + mx_all
    mx_end = mx_all[length - 1:length]
    m_new = b_all[length - 1:length] + mx_end
    w_old_all = jnp.exp(m_prev - mx_end)
    c_t = c_all.T
    ones = jnp.ones((length, dh), F32)
    j_b = jnp.full((dh, dh), 1.0 / dh, BF16)
    scale = MLSTM_DH ** -0.5
    yield
    k_ts = [(pa_ref[:, col(1, h)] * scale).T for h in heads]
    qbs = [_bf(pa_ref[:, col(0, h)]) for h in heads]
    s_raw = [_dot(qbs[h], _bf(k_ts[h])) for h in heads]
    yield
    inter = [_dot(qbs[h], _bf(cns[h])) for h in heads]
    yield
    mx_bs = [jnp.broadcast_to(mx_all[:, h:h + 1], (length, dh)) for h in heads]
    es = [jnp.exp(jnp.where(causal, c_t[h:h + 1, :] - mx_bs[h][:, 0:length], -jnp.inf))
          for h in heads]
    yield
    v1s = [_bf(jnp.concatenate([pa_ref[:, col(2, h)], ones], axis=1)) for h in heads]
    tots = []
    for h in heads:
        w_inter = jnp.exp(m_prev[:, h:h + 1] - mx_bs[h])
        tots.append(jnp.concatenate([w_inter, w_inter], axis=1) * inter[h]
                    + _dot(_bf(s_raw[h] * es[h]), v1s[h]))
    yield
    new_cn = [w_old_all[:, h:h + 1] * cns[h]
              + _dot(_bf(k_ts[h] * es[h][length - 1:length, :]), v1s[h]) for h in heads]
    yield
    cens = [tots[h][:, 0:dh] - _lane_mean(tots[h][:, 0:dh], j_b) for h in heads]
    yield
    vars_ = [_lane_mean(cens[h] * cens[h], j_b) for h in heads]
    yield
    outs = []
    for h in heads:
        mt_b = jnp.broadcast_to(mt_all[:, h:h + 1], (length, dh))
        inv = 1.0 / jnp.maximum(jnp.abs(tots[h][:, dh:]), jnp.exp(-mt_b))
        hid = cens[h] * inv * lax.rsqrt(vars_[h] * inv * inv + EPS) * norm_ref[:, sls[h]]
        outs.append(_bf(hid * _sigmoid(pa_ref[:, col(3, h)]) * _silu(pa_ref[:, col(4, h)])))

    def store():
        y_ref[...] = jnp.concatenate(outs, axis=1)
        for h in heads:
            cn_scr[h] = new_cn[h]
        m_scr[...] = m_new

    stores.append(store)


def _hgrn_stages(pb_ref, lb_ref, norm_ref, y_ref, st_scr, stores, *, length):
    dk = HGRN_DK
    sub = min(HGRN_SUB, length)
    n_sub = length // sub
    heads = range(HGRN_H)
    sls = [slice(h * dk, (h + 1) * dk) for h in heads]
    col = lambda part, h: slice(part * HGRN_W + h * dk, part * HGRN_W + (h + 1) * dk)
    sts = [st_scr[h] for h in heads]

    lb = lb_ref[...]
    fp = pb_ref[:, HGRN_W:2 * HGRN_W]
    f_gate = lb + (1.0 - lb) * _sigmoid(fp)
    k_all = (1.0 - lb) * _sigmoid(-fp)
    tri_b = _bf(jnp.where(_tri(length), 1.0, 0.0))
    a_all = _cumsum_time(jnp.log(f_gate), tri_b)
    lane_l = lax.broadcasted_iota(jnp.int32, (sub, length), 1)
    row_l = lax.broadcasted_iota(jnp.int32, (sub, length), 0)
    yield
    qs = [pb_ref[:, col(0, h)] for h in heads]
    ks = [k_all[:, sl] for sl in sls]
    avs = [a_all[:, sl] for sl in sls]
    a2_all = a_all * LOG2E
    a2s = [a2_all[:, sl] for sl in sls]
    vbs = [_bf(pb_ref[:, col(2, h)]) for h in heads]
    o_inter = [_dot_nt(_bf(qs[h] * jnp.exp(avs[h])), _bf(sts[h])) for h in heads]
    yield
    new_st = []
    for h in heads:
        a_end = avs[h][length - 1:length]
        new_st.append(jnp.exp(a_end) * sts[h]
                      + _dot_tn(vbs[h], _bf(ks[h] * jnp.exp(a_end - avs[h]))))
    yield

    blocks = [[] for _ in heads]
    for i in range(n_sub):
        r0 = i * sub
        blks = []
        for h in heads:
            if i > 0:
                ref_row = avs[h][r0:r0 + 1]
                q_s = _bf(qs[h][r0:r0 + sub] * jnp.exp(avs[h][r0:r0 + sub] - ref_row))
                k_s = jnp.concatenate(
                    [ks[h][0:r0] * jnp.exp(ref_row - avs[h][0:r0]),
                     jnp.zeros((length - r0, dk), F32)], axis=0)
                blks.append(_dot_nt(q_s, _bf(k_s)))
            else:
                blks.append(jnp.zeros((sub, length), F32))
        for s_idx in range(sub):
            for h in heads:
                a_i = a2s[h][r0:r0 + sub]
                e = jnp.exp2(jnp.minimum(a_i - a_i[s_idx:s_idx + 1], 0.0))
                col_v = jnp.sum(qs[h][r0:r0 + sub] * e * ks[h][r0 + s_idx:r0 + s_idx + 1],
                                axis=-1, keepdims=True)
                blks[h] = jnp.where(lane_l == r0 + s_idx, col_v, blks[h])
        yield
        for h in heads:
            blocks[h].append(jnp.where(lane_l <= r0 + row_l, blks[h], 0.0))

    outs = []
    for h in heads:
        scores = jnp.concatenate(blocks[h], axis=0) if n_sub > 1 else blocks[h][0]
        o = o_inter[h] + _dot(_bf(scores), vbs[h])
        o = o * lax.rsqrt(jnp.mean(o * o, axis=-1, keepdims=True) + EPS)
        outs.append(_bf(o * norm_ref[:, sls[h]] * _silu(pb_ref[:, col(3, h)])))

    def store():
        y_ref[...] = jnp.concatenate(outs, axis=1)
        for h in heads:
            st_scr[h] = new_st[h]

    stores.append(store)


def _pair_sum(x, even):
    s_even = jnp.sum(jnp.where(even, x, 0.0), axis=-1, keepdims=True)
    s_odd = jnp.sum(jnp.where(even, 0.0, x), axis=-1, keepdims=True)
    return jnp.where(even, s_even, s_odd)


def _rwkv_stages(pc_ref, mu_ref, wa_up_ref, w0_ref, a0_ref, kk_ref, ka_ref, rk_ref, gg_ref, gb_ref,
                 y_ref, bd_scr, carry_scr, stores, *, length):
    dh = RWKV_DH
    w = RWKV_W
    pairs = range(RWKV_H // 2)
    sls = [slice(p * LANES, (p + 1) * LANES) for p in pairs]
    l2 = 2 * length
    bds = [bd_scr[p] for p in pairs]
    gn_g = gg_ref[...]
    gn_b = gb_ref[...]

    pc = pc_ref[:, 0:C_COLS]
    row0 = lax.broadcasted_iota(jnp.int32, (length, C_COLS), 0) == 0
    prev = jnp.where(row0, carry_scr[...], pltpu.roll(pc, 1, 0))
    xs = pc + mu_ref[...] * (prev - pc)
    r = xs[:, 0:w]
    k = xs[:, w:2 * w]
    v = xs[:, 2 * w:3 * w]
    low = xs[:, 3 * w:3 * w + 2 * RWKV_RANK]
    z = xs[:, 3 * w + 2 * RWKV_RANK:]
    lane = lax.broadcasted_iota(jnp.int32, (length, 2 * RWKV_RANK), 1)
    low = jnp.where(lane < RWKV_RANK, jnp.tanh(low), low)
    up = _dot(_bf(low), wa_up_ref[...])
    yield
    w_logit = -_softplus(-(w0_ref[...] + up[:, 0:w])) - 0.5
    log_w = -jnp.exp(w_logit)
    a = _sigmoid(a0_ref[...] + up[:, w:2 * w])
    kk_raw = k * kk_ref[...]
    k2 = k * (1.0 + (a - 1.0) * ka_ref[...])
    rk_bonus = r * k2 * rk_ref[...]
    tri_b = _bf(jnp.where(_tri(length), 1.0, 0.0))
    cum = _cumsum_time(log_w, tri_b)
    yield
    p_in = jnp.exp(cum)
    p_ex = jnp.exp(cum - log_w)
    p_inv = jnp.exp(-cum)
    cum_end = cum[length - 1:length]
    p_end = jnp.exp(cum_end - cum)
    dec_end = jnp.exp(cum_end)

    even = lax.broadcasted_iota(jnp.int32, (length, LANES), 1) < dh
    rows2 = lax.broadcasted_iota(jnp.int32, (l2, l2), 0)
    cols2 = lax.broadcasted_iota(jnp.int32, (l2, l2), 1)
    same = (rows2 >= length) == (cols2 >= length)
    strict_bd = same & (rows2 > cols2)
    incl_bd = same & (rows2 >= cols2)
    rows_s = lax.broadcasted_iota(jnp.int32, (LANES, LANES), 0)
    cols_s = lax.broadcasted_iota(jnp.int32, (LANES, LANES), 1)
    state_bd = (rows_s >= dh) == (cols_s >= dh)
    n_double = int(math.log2(length))
    assert 2 ** n_double == length

    def stack(x):
        return jnp.concatenate([jnp.where(even, x, 0.0), jnp.where(even, 0.0, x)], axis=0)

    kas, lhss, grams = [], [], []
    for p in pairs:
        sl = sls[p]
        kk_r = kk_raw[:, sl]
        kk = kk_r / jnp.maximum(jnp.sqrt(_pair_sum(kk_r * kk_r, even)), 1e-12)
        ka = kk * a[:, sl]
        lhs = _bf(jnp.concatenate([stack(-kk * p_ex[:, sl]), stack(r[:, sl] * p_in[:, sl])], axis=0))
        bh = _bf(ka * p_inv[:, sl])
        kh = _bf(k2[:, sl] * p_inv[:, sl])
        rhs = jnp.concatenate([bh, bh, kh, kh], axis=0)
        grams.append(_dot_nt(lhs, rhs))
        kas.append(ka)
        lhss.append(lhs)
    yield
    uy0s = [_dot_nt(lhss[p], _bf(bds[p])) for p in pairs]
    v_stacks = [_bf(stack(v[:, sl])) for sl in sls]
    yield
    n_mats = [jnp.where(strict_bd, grams[p][0:l2, 0:l2], 0.0) for p in pairs]
    xs_u = [uy0s[p][0:l2] + _dot(_bf(jnp.where(strict_bd, grams[p][0:l2, l2:], 0.0)), v_stacks[p])
            for p in pairs]
    abks = [_bf(jnp.concatenate([jnp.where(incl_bd, grams[p][l2:, 0:l2], 0.0),
                                 jnp.where(incl_bd, grams[p][l2:, l2:], 0.0)], axis=1))
            for p in pairs]
    yield
    for step in range(n_double):
        for p in pairs:
            n_b = _bf(n_mats[p])
            xs_u[p] = xs_u[p] + _dot(n_b, _bf(xs_u[p]))
            if step + 1 < n_double:
                n_mats[p] = _dot(n_b, n_b)
        yield

    new_bd, ys = [], []
    for p in pairs:
        sl = sls[p]
        u_st = xs_u[p]
        y_st = uy0s[p][l2:] + _dot(abks[p], jnp.concatenate([_bf(u_st), v_stacks[p]], axis=0))
        ys.append(y_st[0:length] + y_st[length:])
        u_pair = u_st[0:length] + u_st[length:]
        upd = _dot_tn(_bf(jnp.concatenate([u_pair, v[:, sl]], axis=0)),
                      _bf(jnp.concatenate([kas[p] * p_end[:, sl], k2[:, sl] * p_end[:, sl]], axis=0)))
        new_bd.append(bds[p] * dec_end[:, sl] + jnp.where(state_bd, upd, 0.0))
    yield
    outs = []
    for p in pairs:
        sl = sls[p]
        mu = _pair_sum(ys[p], even) * (1.0 / dh)
        cen = ys[p] - mu
        var = _pair_sum(cen * cen, even) * (1.0 / dh)
        y = cen * lax.rsqrt(var + RWKV_GN_EPS) * gn_g[:, sl] + gn_b[:, sl]
        bonus = _pair_sum(rk_bonus[:, sl], even) * v[:, sl]
        outs.append(_bf((y + bonus) * _silu(z[:, sl])))

    def store():
        y_ref[...] = jnp.concatenate(outs, axis=1)
        carry_scr[...] = pc[length - 1:length, :]
        for p in pairs:
            bd_scr[p] = new_bd[p]

    stores.append(store)


def _branch_kernel(pa_ref, pb_ref, pc_ref, bias_i_ref, bias_f_ref, anorm_ref, lb_ref, hnorm_ref,
                   mu_ref, wa_up_ref, w0_ref, a0_ref, kk_ref, ka_ref, rk_ref, gg_ref, gb_ref,
                   c0_ref, n0_ref, m0_ref, hs0_ref, rs0_ref, shift0_ref,
                   ya_ref, yb_ref, yc_ref, c_out, n_out, m_out, hs_out, rs_out, shift_out,
                   cn_scr, m_scr, st_scr, bd_scr, carry_scr, *, length, n_grp):
    c_idx = pl.program_id(1)
    dh = MLSTM_DH
    rh = RWKV_DH
    streams = range(n_grp)

    def rows(ref, g):
        return ref.at[g] if len(ref.shape) == 3 else ref.at[pl.ds(g * length, length)]

    @pl.when(c_idx == 0)
    def _():
        for g in streams:
            n_t = n0_ref[g].T
            for h in range(MLSTM_H):
                cn_scr[g, h, :, 0:dh] = c0_ref[g, h].T
                cn_scr[g, h, :, dh:2 * dh] = jnp.broadcast_to(n_t[:, h:h + 1], (dh, dh))
            m_scr[g] = m0_ref[g]
            for h in range(HGRN_H):
                st_scr[g, h] = hs0_ref[g, h].T
            zero = jnp.zeros((rh, rh), F32)
            for p in range(RWKV_H // 2):
                top = jnp.concatenate([rs0_ref[g, 2 * p], zero], axis=1)
                bot = jnp.concatenate([zero, rs0_ref[g, 2 * p + 1]], axis=1)
                bd_scr[g, p] = jnp.concatenate([top, bot], axis=0)
            carry_scr[g] = shift0_ref[g]

    stores = []
    active = (
        [_rwkv_stages(rows(pc_ref, g), mu_ref, wa_up_ref, w0_ref, a0_ref, kk_ref, ka_ref, rk_ref,
                      gg_ref, gb_ref, yc_ref.at[g], bd_scr.at[g], carry_scr.at[g], stores,
                      length=length) for g in streams]
        + [_mlstm_stages(rows(pa_ref, g), rows(pc_ref, g), bias_i_ref, bias_f_ref, anorm_ref,
                         ya_ref.at[g], cn_scr.at[g], m_scr.at[g], stores, length=length)
           for g in streams]
        + [_hgrn_stages(rows(pb_ref, g), lb_ref, hnorm_ref, yb_ref.at[g], st_scr.at[g], stores,
                        length=length) for g in streams])
    while active:
        for gen in list(active):
            if next(gen, StopIteration) is StopIteration:
                active.remove(gen)
    for store in stores:
        store()

    @pl.when(c_idx == pl.num_programs(1) - 1)
    def _():
        for g in streams:
            for h in range(MLSTM_H):
                c_out[g, h] = cn_scr[g, h, :, 0:dh].T
                n_out[g, h:h + 1, :] = cn_scr[g, h, :, dh:2 * dh].T[0:1, :]
            m_out[g] = m_scr[g]
            for h in range(HGRN_H):
                hs_out[g, h] = st_scr[g, h].T
            for p in range(RWKV_H // 2):
                rs_out[g, 2 * p] = bd_scr[g, p, 0:rh, 0:rh]
                rs_out[g, 2 * p + 1] = bd_scr[g, p, rh:2 * rh, rh:2 * rh]
            shift_out[g] = carry_scr[g]


def _branch_call(pa, pb, pc, p, states, layer, state_layer, *, n_seq, n_chunks, length, row_off,
                 name):
    grp = BRANCH_GROUP
    assert n_seq % grp == 0
    if n_chunks == 1:
        assert row_off % (grp * length) == 0
        blk0 = row_off // (grp * length)
        views = (pa, pb, pc)
        row_spec = lambda width: pl.BlockSpec((grp * length, width), lambda b, c: (blk0 + b, 0))
    else:
        assert row_off == 0 and pa.shape[0] == n_seq * n_chunks * length
        views = tuple(a.reshape(n_seq, n_chunks * length, a.shape[1]) for a in (pa, pb, pc))
        row_spec = lambda width: pl.BlockSpec((grp, length, width), lambda b, c: (b, c, 0))
    out_spec = lambda width: pl.BlockSpec((grp, length, width), lambda b, c: (b, c, 0))
    state_tails = [(MLSTM_H, MLSTM_DH, MLSTM_DH), (MLSTM_H, MLSTM_DH), (1, LANES),
                   (HGRN_H, HGRN_DK, HGRN_DV), (RWKV_H, RWKV_DH, RWKV_DH), (1, C_COLS)]
    state_in = [pl.BlockSpec((None, grp) + t, lambda b, c, t=t: (state_layer, b) + (0,) * len(t))
                for t in state_tails]
    state_out = [pl.BlockSpec((grp,) + t, lambda b, c, t=t: (b,) + (0,) * len(t))
                 for t in state_tails]
    state_shapes = [jax.ShapeDtypeStruct((n_seq,) + t, F32) for t in state_tails]
    params = [p['bias_i'], p['bias_f'], p['mlstm_norm'], p['lb'], p['hgrn_norm'], p['mu'],
              p['wa_up'], p['w0'], p['a0'], p['k_k'], p['k_a'], p['r_k'], p['gn_g'], p['gn_b']]
    t_len = n_chunks * length
    outs = pl.pallas_call(
        functools.partial(_branch_kernel, length=length, n_grp=grp),
        grid=(n_seq // grp, n_chunks),
        in_specs=[row_spec(A_W), row_spec(B_W), row_spec(C_W)]
        + [_layer_spec(a, layer) for a in params] + state_in,
        out_specs=[out_spec(MLSTM_W), out_spec(HGRN_W), out_spec(RWKV_W)] + state_out,
        out_shape=[jax.ShapeDtypeStruct((n_seq, t_len, MLSTM_W), BF16),
                   jax.ShapeDtypeStruct((n_seq, t_len, HGRN_W), BF16),
                   jax.ShapeDtypeStruct((n_seq, t_len, RWKV_W), BF16)] + state_shapes,
        scratch_shapes=[pltpu.VMEM((grp, MLSTM_H, MLSTM_DH, 2 * MLSTM_DH), F32),
                        pltpu.VMEM((grp, 1, LANES), F32),
                        pltpu.VMEM((grp, HGRN_H, HGRN_DV, HGRN_DK), F32),
                        pltpu.VMEM((grp, RWKV_H // 2, 2 * RWKV_DH, 2 * RWKV_DH), F32),
                        pltpu.VMEM((grp, 1, C_COLS), F32)],
        compiler_params=pltpu.CompilerParams(
            dimension_semantics=("parallel", "arbitrary"), vmem_limit_bytes=VMEM_LIMIT),
        name=name,
    )(*views, *params, *states)
    ys = tuple(y.reshape(n_seq * t_len, y.shape[2]) for y in outs[0:3])
    return ys, outs[3:]


def _layer_weights(w):
    a_cols = 5 * MLSTM_W + 2 * MLSTM_H
    if0 = 3 * MLSTM_W
    b0 = a_cols
    c0 = b0 + B_W
    g0 = c0 + C_COLS
    w_a = _bf(jnp.concatenate([w[:, 0:if0], w[:, if0 + 2 * MLSTM_H:a_cols]], axis=1))
    w_b = _bf(w[:, b0:c0])
    gate_pad = jnp.zeros((D_MODEL, LANES - MLSTM_H), F32)
    w_c = _bf(jnp.concatenate(
        [w[:, c0:g0], w[:, if0:if0 + MLSTM_H], gate_pad,
         w[:, if0 + MLSTM_H:if0 + 2 * MLSTM_H], gate_pad,
         jnp.zeros((D_MODEL, LANES), F32)], axis=1))
    n_chunk = D_MODEL // MERGE_NC
    w_g = _bf(jnp.concatenate(
        [w[:, g0 + b * D_MODEL + j * MERGE_NC:g0 + b * D_MODEL + (j + 1) * MERGE_NC]
         for j in range(n_chunk) for b in range(3)], axis=1))
    return w_a, w_b, w_c, w_g


def _lane_pad(m):
    pad = [(0, 0)] * (m.ndim - 1) + [(0, LANES - m.shape[-1])]
    return jnp.expand_dims(jnp.pad(m.astype(F32), pad), -2)


def _stacked_params(lb_all, norm_pre, norm_post, mlstm_b_i, mlstm_b_f, mlstm_norm, hgrn_norm,
                    rwkv_mu, rwkv_w0, rwkv_w_up, rwkv_a0, rwkv_a_up, rwkv_k_k, rwkv_k_a, rwkv_r_k,
                    rwkv_gn_g, rwkv_gn_b, w_proj_a, w_proj_b, w_proj_c, w_out):
    row = lambda a: a.astype(F32)[:, None, :]
    zero = jnp.zeros(rwkv_w_up.shape, F32)
    wa_up = _bf(jnp.concatenate([jnp.concatenate([rwkv_w_up, zero], axis=2),
                                 jnp.concatenate([zero, rwkv_a_up], axis=2)], axis=1))
    return dict(
        norm_pre=row(norm_pre), norm_post=row(norm_post), bias_i=_lane_pad(mlstm_b_i),
        bias_f=_lane_pad(mlstm_b_f), mlstm_norm=row(mlstm_norm), lb=row(lb_all),
        hgrn_norm=row(hgrn_norm), mu=row(rwkv_mu), wa_up=wa_up, w0=row(rwkv_w0), a0=row(rwkv_a0),
        k_k=row(rwkv_k_k), k_a=row(rwkv_k_a), r_k=row(rwkv_r_k), gn_g=row(rwkv_gn_g),
        gn_b=row(rwkv_gn_b), wpa=_bf(w_proj_a), wpb=_bf(w_proj_b), wpc=_bf(w_proj_c),
        wo=_bf(w_out))


def _branches(p, layer, proj_main, proj_tail, st_s, *, bp, t_p, bs, t_s):
    tails = [(MLSTM_H, MLSTM_DH, MLSTM_DH), (MLSTM_H, MLSTM_DH), (1, LANES),
             (HGRN_H, HGRN_DK, HGRN_DV), (RWKV_H, RWKV_DH, RWKV_DH), (1, C_COLS)]
    st_zero = tuple(jnp.zeros((1, bp) + t, F32) for t in tails)
    y_meta, st_meta = _branch_call(*proj_tail, p, st_zero, layer, 0, n_seq=bp, n_chunks=1,
                                   length=N_META, row_off=0, name=f'branch_meta{layer}')
    y_main, st_p = _branch_call(*proj_main, p, tuple(s[None] for s in st_meta), layer, 0,
                                n_seq=bp, n_chunks=t_p // CHUNK, length=CHUNK, row_off=0,
                                name=f'branch_main{layer}')
    y_samp, st_so = _branch_call(*proj_tail, p, st_s, layer, layer, n_seq=bs, n_chunks=1,
                                 length=t_s, row_off=bp * N_META, name=f'branch_samp{layer}')
    y_tail = tuple(jnp.concatenate([a, b], axis=0) for a, b in zip(y_meta, y_samp))
    unpad = lambda st: (st[0], st[1], st[2][:, 0, 0:MLSTM_H]) + tuple(st[3:])
    return y_main, y_tail, unpad(st_p), unpad(st_so)


def kernel(x_prompt, x_sample, state_mlstm_C, state_mlstm_n, state_mlstm_m, state_hgrn_S,
           state_rwkv_S, cache_rwkv_shift, meta_tokens, norm_pre, norm_post, w_in,
           mlstm_b_i, mlstm_b_f, mlstm_norm, hgrn_lb_logits, hgrn_norm, rwkv_mu, rwkv_w0,
           rwkv_w_up, rwkv_a0, rwkv_a_up, rwkv_k_k, rwkv_k_a, rwkv_r_k, rwkv_gn_g, rwkv_gn_b,
           w_proj_a, w_proj_b, w_proj_c, w_out):
    bp, t_p, _ = x_prompt.shape
    bs, t_s, _ = x_sample.shape
    depth = w_in.shape[0]
    assert t_p % CHUNK == 0 and t_s % HGRN_SUB == 0 and (bp * N_META) % (BRANCH_GROUP * t_s) == 0

    sm = jax.nn.softmax(hgrn_lb_logits.astype(F32), axis=0)
    lb_all = jnp.cumsum(sm, axis=0) - sm[0]

    x_main = x_prompt.reshape(bp * t_p, D_MODEL)
    meta = jnp.broadcast_to(meta_tokens.astype(F32)[None], (bp, N_META, D_MODEL))
    x_tail = jnp.concatenate([meta.reshape(bp * N_META, D_MODEL),
                              x_sample.reshape(bs * t_s, D_MODEL)], axis=0)

    p = _stacked_params(lb_all, norm_pre, norm_post, mlstm_b_i, mlstm_b_f, mlstm_norm, hgrn_norm,
                        rwkv_mu, rwkv_w0, rwkv_w_up, rwkv_a0, rwkv_a_up, rwkv_k_k, rwkv_k_a,
                        rwkv_r_k, rwkv_gn_g, rwkv_gn_b, w_proj_a, w_proj_b, w_proj_c, w_out)
    st_s = (state_mlstm_C.astype(F32), state_mlstm_n.astype(F32), _lane_pad(state_mlstm_m),
            state_hgrn_S.astype(F32), state_rwkv_S.astype(F32), cache_rwkv_shift.astype(F32))

    outs_p, outs_s = [], []
    for l in range(depth):
        weights = _layer_weights(w_in[l])
        names = ('w_a', 'w_b', 'w_c')
        proj_main = tuple(_proj(x_main, p['norm_pre'], l, w, f'proj_{k}_main{l}')
                          for k, w in zip(names, weights[0:3]))
        proj_tail = tuple(_proj(x_tail, p['norm_pre'], l, w, f'proj_{k}_tail{l}')
                          for k, w in zip(names, weights[0:3]))
        y_main, y_tail, st_p_out, st_s_out = _branches(
            p, l, proj_main, proj_tail, st_s, bp=bp, t_p=t_p, bs=bs, t_s=t_s)
        merge_args = (weights[3], p['wpa'], p['wpb'], p['wpc'], p['wo'], p['norm_post'], l)
        x_main = _merge(x_main, p['norm_pre'], *y_main, *merge_args, f'merge_main{l}')
        x_tail = _merge(x_tail, p['norm_pre'], *y_tail, *merge_args, f'merge_tail{l}')
        outs_p.append(st_p_out)
        outs_s.append(st_s_out)

    states_p = tuple(jnp.stack([o[j] for o in outs_p]) for j in range(6))
    states_s = tuple(jnp.stack([o[j] for o in outs_s]) for j in range(6))
    y_prompt = x_main.reshape(bp, t_p, D_MODEL)
    y_sample = x_tail[bp * N_META:].reshape(bs, t_s, D_MODEL)
    return (y_prompt, y_sample) + states_p + states_s
```
